```python
import math
import jax, jax.numpy as jnp
from jax import lax
import numpy as np

D_MODEL = 1024
BATCH = 4
SEQ = 4096
DEPTH = 4

GRID_W = 64
CTX_LEN = 256
HEAD_DIM = 64
A_HEADS = 4
A_KV_HEADS = 2
B_HEADS = 4
B_QK_DIM = HEAD_DIM // 2
C_HEADS = 4
NA_KH = 8
NA_KW = 16
S5_GROUPS = 16
S5_GROUP_CH = 16
S5_STATE = 64
S5_DT_MIN = 1e-3
S5_DT_MAX = 1e-1

A_WIDTH = A_HEADS * HEAD_DIM
A_KV_WIDTH = A_KV_HEADS * HEAD_DIM
B_WIDTH = B_HEADS * HEAD_DIM
C_WIDTH = C_HEADS * HEAD_DIM
S5_WIDTH = S5_GROUPS * S5_GROUP_CH
MIX_WIDTH = A_WIDTH + B_WIDTH + C_WIDTH + S5_WIDTH
IN_SIZES = (A_WIDTH, A_KV_WIDTH, A_KV_WIDTH, A_WIDTH,
            B_WIDTH, B_WIDTH, B_WIDTH, B_WIDTH,
            C_WIDTH, C_WIDTH, C_WIDTH, C_WIDTH,
            S5_WIDTH, S5_WIDTH)
IN_WIDTH = sum(IN_SIZES)
IN_SPLITS = tuple(int(s) for s in np.cumsum(IN_SIZES)[:-1])

Q_BLOCK = 128
ROPE_BASE = 10000.0
RMS_EPS = 1e-6
LN_EPS = 1e-5
DEEPNORM_ALPHA = (2 * DEPTH) ** 0.25
DEEPNORM_BETA = (8 * DEPTH) ** -0.25

kernel_name = "hybrid_parallel_heads_dit_block"


def rms_norm(x, g):
    xf = x.astype(jnp.float32)
    y = xf * lax.rsqrt(jnp.mean(xf * xf, axis=-1, keepdims=True) + RMS_EPS)
    return (y * g).astype(x.dtype)


def layer_norm(x, g, b):
    xf = x.astype(jnp.float32)
    mu = jnp.mean(xf, axis=-1, keepdims=True)
    var = jnp.mean(jnp.square(xf - mu), axis=-1, keepdims=True)
    return ((xf - mu) * lax.rsqrt(var + LN_EPS) * g + b).astype(x.dtype)


def softmax32(s):
    return jax.nn.softmax(s.astype(jnp.float32), axis=-1)


def rope_1d(x, pos):
    half = x.shape[-1] // 2
    inv = ROPE_BASE ** (-jnp.arange(half, dtype=jnp.float32) / half)
    ang = pos.astype(jnp.float32)[:, None] * inv
    cos, sin = jnp.cos(ang).astype(x.dtype), jnp.sin(ang).astype(x.dtype)
    x1, x2 = x[..., :half], x[..., half:]
    return jnp.concatenate([x1 * cos - x2 * sin, x1 * sin + x2 * cos], axis=-1)


def rope_2d(x, row, col):
    h = x.shape[-1] // 2
    return jnp.concatenate([rope_1d(x[..., :h], row), rope_1d(x[..., h:], col)], axis=-1)


def split_heads(t, n):
    b, s, _ = t.shape
    return t.reshape(b, s, n, -1).transpose(0, 2, 1, 3)


def merge_heads(t):
    b, n, s, d = t.shape
    return t.transpose(0, 2, 1, 3).reshape(b, s, n * d)


def sweep_query_blocks(fn, qs):
    def to_blocks(q):
        qb = q.reshape(q.shape[:-2] + (q.shape[-2] // Q_BLOCK, Q_BLOCK, q.shape[-1]))
        return jnp.moveaxis(qb, -3, 0)
    out = lax.map(fn, tuple(to_blocks(q) for q in qs))
    out = jnp.moveaxis(out, 0, -3)
    return out.reshape(out.shape[:-3] + (-1, out.shape[-1]))


def attend(q, k, v, scale):
    p = softmax32(jnp.einsum('bkrqd,bktd->bkrqt', q, k) * scale).astype(v.dtype)
    return jnp.einsum('bkrqt,bktd->bkrqd', p, v)


def gqa_mixer(q_l, k_l, v_l, q_c, k_c, v_c, qn_g, kn_g, row, col, need_ctx):
    rep = A_HEADS // A_KV_HEADS
    scale = HEAD_DIM ** -0.5
    q = rope_2d(rms_norm(split_heads(q_l, A_HEADS), qn_g), row, col)
    k = rope_2d(rms_norm(split_heads(k_l, A_KV_HEADS), kn_g), row, col)
    v = split_heads(v_l, A_KV_HEADS)
    kc = rms_norm(split_heads(k_c, A_KV_HEADS), kn_g)
    vc = split_heads(v_c, A_KV_HEADS)
    k_all = jnp.concatenate([k, kc], axis=2)
    v_all = jnp.concatenate([v, vc], axis=2)

    def group(t):
        b, _, s, d = t.shape
        return t.reshape(b, A_KV_HEADS, rep, s, d)

    def ungroup(o):
        return o.reshape(o.shape[0], A_HEADS, o.shape[3], o.shape[4])

    o = sweep_query_blocks(lambda qs: attend(qs[0], k_all, v_all, scale), (group(q),))
    y_lat = merge_heads(ungroup(o))
    y_ctx = None
    if need_ctx:
        qc = group(rms_norm(split_heads(q_c, A_HEADS), qn_g))
        y_ctx = merge_heads(ungroup(attend(qc, kc, vc, scale)))
    return y_lat, y_ctx


def diff_attend(q1, q2, k1, k2, v, lam, scale):
    p1 = softmax32(jnp.einsum('bhqd,bhtd->bhqt', q1, k1) * scale)
    p2 = softmax32(jnp.einsum('bhqd,bhtd->bhqt', q2, k2) * scale)
    return jnp.einsum('bhqt,bhtd->bhqd', (p1 - lam * p2).astype(v.dtype), v)


def diff_mixer(q_l, k_l, v_l, q_c, k_c, v_c, lq1, lk1, lq2, lk2, subln_g, lam_init, row, col, need_ctx):
    scale = B_QK_DIM ** -0.5
    f32 = jnp.float32
    q = split_heads(q_l, B_HEADS)
    k = split_heads(k_l, B_HEADS)
    v = split_heads(v_l, B_HEADS)
    q1, q2 = rope_2d(q[..., :B_QK_DIM], row, col), rope_2d(q[..., B_QK_DIM:], row, col)
    k1, k2 = rope_2d(k[..., :B_QK_DIM], row, col), rope_2d(k[..., B_QK_DIM:], row, col)
    qc = split_heads(q_c, B_HEADS)
    kc = split_heads(k_c, B_HEADS)
    vc = split_heads(v_c, B_HEADS)
    kc1, kc2 = kc[..., :B_QK_DIM], kc[..., B_QK_DIM:]
    k1_all = jnp.concatenate([k1, kc1], axis=2)
    k2_all = jnp.concatenate([k2, kc2], axis=2)
    v_all = jnp.concatenate([v, vc], axis=2)
    lam = (jnp.exp(jnp.sum(lq1.astype(f32) * lk1.astype(f32)))
           - jnp.exp(jnp.sum(lq2.astype(f32) * lk2.astype(f32))) + lam_init)

    def post(o):
        return merge_heads(rms_norm(o, subln_g) * (1.0 - lam_init))

    o = sweep_query_blocks(
        lambda qs: diff_attend(qs[0], qs[1], k1_all, k2_all, v_all, lam, scale), (q1, q2))
    y_lat = post(o)
    y_ctx = None
    if need_ctx:
        y_ctx = post(diff_attend(qc[..., :B_QK_DIM], qc[..., B_QK_DIM:], kc1, kc2, vc, lam, scale))
    return y_lat, y_ctx


def na_mixer(q_l, k_l, v_l, q_c, k_c, v_c, rpb, need_ctx):
    bsz, seq, _ = q_l.shape
    rows = seq // GRID_W
    kh = min(NA_KH, rows)
    scale = HEAD_DIM ** -0.5
    q = split_heads(q_l, C_HEADS)
    k = split_heads(k_l, C_HEADS)
    v = split_heads(v_l, C_HEADS)
    kc = split_heads(k_c, C_HEADS)
    vc = split_heads(v_c, C_HEADS)

    def grid(t):
        return t.reshape(t.shape[0], t.shape[1], rows, GRID_W, t.shape[-1])

    qg, kg, vg = grid(q), grid(k), grid(v)
    wcol = jnp.arange(GRID_W)
    col_start = jnp.clip(wcol - NA_KW // 2, 0, GRID_W - NA_KW)
    col_idx = col_start[:, None] + jnp.arange(NA_KW)[None, :]
    dc_idx = col_idx - wcol[:, None] + (NA_KW - 1)

    def row_fn(args):
        r, q_row = args
        rs = jnp.clip(r - kh // 2, 0, rows - kh)
        k_win = lax.dynamic_slice_in_dim(kg, rs, kh, axis=2)[:, :, :, col_idx]
        v_win = lax.dynamic_slice_in_dim(vg, rs, kh, axis=2)[:, :, :, col_idx]
        dr_idx = rs + jnp.arange(kh) - r + (NA_KH - 1)
        bias = rpb[:, dr_idx[None, :, None], dc_idx[:, None, :]]
        s_win = jnp.einsum('bhwd,bhiwjd->bhwij', q_row, k_win) * scale + bias
        s_ctx = jnp.einsum('bhwd,bhtd->bhwt', q_row, kc) * scale
        b_, h_, w_ = s_win.shape[:3]
        p = softmax32(jnp.concatenate([s_win.reshape(b_, h_, w_, kh * NA_KW), s_ctx], axis=-1)).astype(v.dtype)
        p_win = p[..., :kh * NA_KW].reshape(b_, h_, w_, kh, NA_KW)
        p_ctx = p[..., kh * NA_KW:]
        return (jnp.einsum('bhwij,bhiwjd->bhwd', p_win, v_win)
                + jnp.einsum('bhwt,bhtd->bhwd', p_ctx, vc))

    out = lax.map(row_fn, (jnp.arange(rows), jnp.moveaxis(qg, 2, 0)))
    out = jnp.moveaxis(out, 0, 2).reshape(q.shape)
    y_lat = merge_heads(out)
    y_ctx = None
    if need_ctx:
        y_ctx = merge_heads(attend(split_heads(q_c, C_HEADS)[:, :, None], kc, vc, scale)[:, :, 0])
    return y_lat, y_ctx


def s5_discretise(a_re, a_im, log_dt, b_re, b_im):
    f32 = jnp.float32
    lam = lax.complex(a_re.astype(f32), a_im.astype(f32))
    dt = jnp.exp(log_dt.astype(f32))[:, None]
    lam_bar = jnp.exp(lam * dt)
    b = lax.complex(b_re.astype(f32), b_im.astype(f32))
    b_bar = ((lam_bar - 1.0) / lam)[..., None] * b
    return lam_bar, b_bar


def _ssm_combine(e1, e2):
    a1, b1 = e1
    a2, b2 = e2
    return a1 * a2, a2 * b1 + b2


def s5_scan(u, lam_bar, b_bar, h0, reverse):
    bu = jnp.einsum('gpc,blgc->blgp', b_bar, u.astype(jnp.complex64))
    if reverse:
        bu = jnp.flip(bu, axis=1)
    if h0 is not None:
        bu = bu.at[:, 0].add(lam_bar * h0)
    a = jnp.broadcast_to(lam_bar, bu.shape)
    _, h = lax.associative_scan(_ssm_combine, (a, bu), axis=1)
    h_last = h[:, -1]
    if reverse:
        h = jnp.flip(h, axis=1)
    return h, h_last


def s5_mixer(u_l, u_c, a_re, a_im, log_dt, b_re, b_im, c_re, c_im, d_skip, w_glu, need_ctx):
    f32 = jnp.float32

    def to_groups(u):
        return u.reshape(u.shape[0], u.shape[1], S5_GROUPS, S5_GROUP_CH)

    ug, ucg = to_groups(u_l), to_groups(u_c)
    disc_f = s5_discretise(a_re[0], a_im[0], log_dt[0], b_re[0], b_im[0])
    disc_b = s5_discretise(a_re[1], a_im[1], log_dt[1], b_re[1], b_im[1])
    c_f = lax.complex(c_re[0].astype(f32), c_im[0].astype(f32))
    c_b = lax.complex(c_re[1].astype(f32), c_im[1].astype(f32))

    def readout(h_f, h_b, u):
        y = (jnp.einsum('gcp,blgp->blgc', c_f, h_f).real
             + jnp.einsum('gcp,blgp->blgc', c_b, h_b).real)
        y = y.reshape(u.shape).astype(u.dtype) + d_skip * u
        z = jax.nn.gelu(y) @ w_glu
        z_val, z_gate = jnp.split(z, 2, axis=-1)
        return z_val * jax.nn.sigmoid(z_gate)

    hc_f, hf_last = s5_scan(ucg, disc_f[0], disc_f[1], None, False)
    hc_b, hb_last = s5_scan(ucg, disc_b[0], disc_b[1], None, True)
    hl_f, _ = s5_scan(ug, disc_f[0], disc_f[1], hf_last, False)
    hl_b, _ = s5_scan(ug, disc_b[0], disc_b[1], hb_last, True)
    y_lat = readout(hl_f, hl_b, u_l)
    y_ctx = readout(hc_f, hc_b, u_c) if need_ctx else None
    return y_lat, y_ctx


def hybrid_layer(x, ctx, c, c_ctx, row, col, layer_idx, need_ctx,
                 w_ada, b_ada, w_in, w_out, ln_g, ln_b, qn_g, kn_g,
                 lq1, lk1, lq2, lk2, subln_g, rpb,
                 a_re, a_im, log_dt, b_re, b_im, c_re, c_im, d_skip, w_glu):
    silu = jax.nn.silu
    shift, scale, gate = jnp.split(silu(c) @ w_ada + b_ada, 3, axis=-1)
    shift_c, scale_c, gate_c = jnp.split(silu(c_ctx) @ w_ada + b_ada, 3, axis=-1)
    h = x * (1.0 + scale[:, None]) + shift[:, None]
    hc = ctx * (1.0 + scale_c) + shift_c
    (aq, ak, av, ag, bq, bk, bv, bg, cq, ck, cv, cg, du, dg) = jnp.split(h @ w_in, IN_SPLITS, axis=-1)
    (aqc, akc, avc, agc, bqc, bkc, bvc, bgc, cqc, ckc, cvc, cgc, duc, dgc) = jnp.split(
        hc @ w_in, IN_SPLITS, axis=-1)
    lam_init = 0.8 - 0.6 * math.exp(-0.3 * layer_idx)

    ya, ya_c = gqa_mixer(aq, ak, av, aqc, akc, avc, qn_g, kn_g, row, col, need_ctx)
    yb, yb_c = diff_mixer(bq, bk, bv, bqc, bkc, bvc, lq1, lk1, lq2, lk2, subln_g, lam_init, row, col, need_ctx)
    yc, yc_c = na_mixer(cq, ck, cv, cqc, ckc, cvc, rpb, need_ctx)
    yd, yd_c = s5_mixer(du, duc, a_re, a_im, log_dt, b_re, b_im, c_re, c_im, d_skip, w_glu, need_ctx)

    y = jnp.concatenate([ya * silu(ag), yb * silu(bg), yc * silu(cg), yd * silu(dg)], axis=-1) @ w_out
    x_new = layer_norm(DEEPNORM_ALPHA * x + gate[:, None] * y, ln_g, ln_b)
    ctx_new = None
    if need_ctx:
        y_c = jnp.concatenate([ya_c * silu(agc), yb_c * silu(bgc), yc_c * silu(cgc), yd_c * silu(dgc)],
                              axis=-1) @ w_out
        ctx_new = layer_norm(DEEPNORM_ALPHA * ctx + gate_c * y_c, ln_g, ln_b)
    return x_new, ctx_new


def setup_inputs(seed: int = 0) -> dict:
    key = jax.random.key(seed)
    ks = jax.random.split(key, 27)
    f32 = jnp.float32

    def nrm(k, shape, s):
        return s * jax.random.normal(k, shape, f32)

    G, P, CH = S5_GROUPS, S5_STATE, S5_GROUP_CH
    return {
        "x": nrm(ks[0], (BATCH, SEQ, D_MODEL), 1.0),
        "c": nrm(ks[1], (BATCH, D_MODEL), 1.0),
        "ctx": nrm(ks[2], (BATCH, CTX_LEN, D_MODEL), 1.0),
        "c_ctx": nrm(ks[3], (D_MODEL,), 1.0),
        "w_ada": nrm(ks[4], (DEPTH, D_MODEL, 3 * D_MODEL), 0.5 * D_MODEL ** -0.5),
        "b_ada": nrm(ks[5], (DEPTH, 3 * D_MODEL), 0.01),
        "w_in": nrm(ks[6], (DEPTH, D_MODEL, IN_WIDTH), D_MODEL ** -0.5),
        "w_out": nrm(ks[7], (DEPTH, MIX_WIDTH, D_MODEL), DEEPNORM_BETA * MIX_WIDTH ** -0.5),
        "ln_g": 1.0 + nrm(ks[8], (DEPTH, D_MODEL), 0.01),
        "ln_b": nrm(ks[9], (DEPTH, D_MODEL), 0.01),
        "qn_g": 1.0 + nrm(ks[10], (DEPTH, HEAD_DIM), 0.01),
        "kn_g": 1.0 + nrm(ks[11], (DEPTH, HEAD_DIM), 0.01),
        "lam_q1": nrm(ks[12], (DEPTH, B_QK_DIM), 0.1),
        "lam_k1": nrm(ks[13], (DEPTH, B_QK_DIM), 0.1),
        "lam_q2": nrm(ks[14], (DEPTH, B_QK_DIM), 0.1),
        "lam_k2": nrm(ks[15], (DEPTH, B_QK_DIM), 0.1),
        "subln_g": 1.0 + nrm(ks[16], (DEPTH, HEAD_DIM), 0.01),
        "na_rpb": nrm(ks[17], (DEPTH, C_HEADS, 2 * NA_KH - 1, 2 * NA_KW - 1), 0.02),
        "s5_a_re": -0.5 + nrm(ks[18], (DEPTH, 2, G, P), 0.01),
        "s5_a_im": math.pi * jnp.arange(P, dtype=f32) + nrm(ks[19], (DEPTH, 2, G, P), 0.01),
        "s5_log_dt": jax.random.uniform(ks[20], (DEPTH, 2, G), f32,
                                        math.log(S5_DT_MIN), math.log(S5_DT_MAX)),
        "s5_b_re": nrm(ks[21], (DEPTH, 2, G, P, CH), (2 * CH) ** -0.5),
        "s5_b_im": nrm(ks[22], (DEPTH, 2, G, P, CH), (2 * CH) ** -0.5),
        "s5_c_re": nrm(ks[23], (DEPTH, 2, G, CH, P), (2 * P) ** -0.5),
        "s5_c_im": nrm(ks[24], (DEPTH, 2, G, CH, P), (2 * P) ** -0.5),
        "s5_d": nrm(ks[25], (DEPTH, S5_WIDTH), 0.5),
        "w_glu": nrm(ks[26], (DEPTH, S5_WIDTH, 2 * S5_WIDTH), S5_WIDTH ** -0.5),
    }


def reference(x, c, ctx, c_ctx, w_ada, b_ada, w_in, w_out, ln_g, ln_b, qn_g, kn_g,
              lam_q1, lam_k1, lam_q2, lam_k2, subln_g, na_rpb,
              s5_a_re, s5_a_im, s5_log_dt, s5_b_re, s5_b_im, s5_c_re, s5_c_im, s5_d, w_glu):
    seq = x.shape[1]
    t = jnp.arange(seq, dtype=jnp.int32)
    row, col = t // GRID_W, t % GRID_W
    for l in range(DEPTH):
        x, ctx = hybrid_layer(
            x, ctx, c, c_ctx, row, col, l, l < DEPTH - 1,
            w_ada[l], b_ada[l], w_in[l], w_out[l], ln_g[l], ln_b[l], qn_g[l], kn_g[l],
            lam_q1[l], lam_k1[l], lam_q2[l], lam_k2[l], subln_g[l], na_rpb[l],
            s5_a_re[l], s5_a_im[l], s5_log_dt[l], s5_b_re[l], s5_b_im[l], s5_c_re[l], s5_c_im[l],
            s5_d[l], w_glu[l])
    return x
```

```python
import functools
import math

import jax
import jax.numpy as jnp
import numpy as np
from jax import lax
from jax.experimental import pallas as pl
from jax.experimental.pallas import tpu as pltpu

F32 = jnp.float32
BF16 = jnp.bfloat16

GRID_WIDTH = 64
HEAD = 64
N_HEADS = 4
A_KV = 2
NA_ROWS = 8
NA_COLS = 16
ROPE_THETA = 10000.0
EPS_RMS = 1e-6
EPS_LN = 1e-5

TOK = 256
KCHUNK = 256
S5_CHUNK = 16
SUBLANES = 8
NEG = -1e30
MIB = 1024 * 1024


def _cparams(n_axes, vmem_mib):
    return pltpu.CompilerParams(dimension_semantics=("parallel",) * n_axes,
                                vmem_limit_bytes=vmem_mib * MIB)


def _dot(a, b):
    return jnp.dot(a, b, preferred_element_type=F32)


def _ada_kernel(c_ref, w_ref, b_ref, o_ref):
    c = c_ref[...]
    s = c * jax.nn.sigmoid(c)
    o_ref[0] = jnp.dot(s, w_ref[0], preferred_element_type=F32,
                       precision=lax.Precision.HIGHEST) + b_ref[0]


def _ada_call(cc, w_ada, b_ada):
    depth, d, d3 = w_ada.shape
    nblk = 3
    bw = d3 // nblk
    return pl.pallas_call(
        _ada_kernel,
        out_shape=jax.ShapeDtypeStruct((depth, SUBLANES, d3), F32),
        grid=(depth, nblk),
        in_specs=[pl.BlockSpec((SUBLANES, d), lambda l, n: (0, 0)),
                  pl.BlockSpec((1, d, bw), lambda l, n: (l, 0, n)),
                  pl.BlockSpec((1, 1, bw), lambda l, n: (l, 0, n))],
        out_specs=pl.BlockSpec((1, SUBLANES, bw), lambda l, n: (l, 0, n)),
        compiler_params=_cparams(2, 32),
        name="ada_mod",
    )(cc, w_ada, b_ada.reshape(depth, 1, d3))


def _swap_halves(x, half):
    out = []
    lane = lax.broadcasted_iota(jnp.int32, (1, 128), 1)
    first = (lane % (2 * half)) < half
    for s in range(x.shape[1] // 128):
        xs = x[:, 128 * s:128 * (s + 1)]
        fwd = pltpu.roll(xs, 128 - half, axis=1)
        bwd = pltpu.roll(xs, half, axis=1)
        out.append(jnp.where(first, fwd, bwd))
    return out[0] if len(out) == 1 else jnp.concatenate(out, axis=1)


def _tile_lanes(t, reps):
    return t if reps == 1 else jnp.concatenate([t] * reps, axis=1)


def _head_meansq(x):
    w = x.shape[1]
    r = lax.broadcasted_iota(jnp.int32, (w, w), 0) // HEAD
    c = lax.broadcasted_iota(jnp.int32, (w, w), 1) // HEAD
    bd = jnp.where(r == c, 1.0, 0.0).astype(BF16)
    xx = x * x
    hi = xx.astype(BF16)
    lo = (xx - hi.astype(F32)).astype(BF16)
    return (_dot(hi, bd) + _dot(lo, bd)) * (1.0 / HEAD)


def _inproj_kernel(x_ref, mod_ref, w_ref, cosa_ref, sina_ref, cosb_ref, sinb_ref, qg_ref, kg_ref,
                   qat_ref, ka_ref, vat_ref, qbt_ref, kb_ref, vbt_ref, qct_ref, kc_ref, vct_ref,
                   gate_ref, du_ref, *, n_batch, d_model):
    b = pl.program_id(0)
    t = pl.program_id(1)
    row = jnp.where(t == 0, n_batch, b)
    mod = mod_ref[pl.ds(row, 1), :]
    shift = mod[:, :d_model]
    scale = mod[:, d_model:2 * d_model]
    h = (x_ref[0] * (1.0 + scale) + shift).astype(BF16)
    z = _dot(h, w_ref[...])

    cosa, sina = cosa_ref[...], sina_ref[...]
    cosb, sinb = cosb_ref[...], sinb_ref[...]

    aq = z[:, 0:256]
    aq = aq * lax.rsqrt(_head_meansq(aq) + EPS_RMS) * _tile_lanes(qg_ref[...], 2)
    aq = aq * _tile_lanes(cosa, 2) + _swap_halves(aq, 16) * _tile_lanes(sina, 2)
    qat_ref[0] = (aq * HEAD ** -0.5).T.astype(BF16)
    ak = z[:, 256:384]
    ak = ak * lax.rsqrt(_head_meansq(ak) + EPS_RMS) * kg_ref[...]
    ak = ak * cosa + _swap_halves(ak, 16) * sina
    ka_ref[0] = ak.astype(BF16)
    vat_ref[0] = z[:, 384:512].T.astype(BF16)

    bq = z[:, 768:1024]
    bq = bq * _tile_lanes(cosb, 2) + _swap_halves(bq, 8) * _tile_lanes(sinb, 2)
    qbt_ref[0] = (bq * (HEAD // 2) ** -0.5).T.astype(BF16)
    bk = z[:, 1024:1280]
    bk = bk * _tile_lanes(cosb, 2) + _swap_halves(bk, 8) * _tile_lanes(sinb, 2)
    kb_ref[0] = bk.astype(BF16)
    vbt_ref[0] = z[:, 1280:1536].T.astype(BF16)

    qct_ref[0] = (z[:, 1792:2048] * HEAD ** -0.5).T.astype(BF16)
    kc_ref[0] = z[:, 2048:2304].astype(BF16)
    vct_ref[0] = z[:, 2304:2560].T.astype(BF16)

    du_ref[0] = z[:, 2816:3072].astype(BF16)
    for i, c0 in enumerate((512, 1536, 2560, 3072)):
        g = z[:, c0:c0 + 256]
        gate_ref[0, :, 256 * i:256 * (i + 1)] = (g * jax.nn.sigmoid(g)).astype(BF16)


def _inproj_call(xs, mod, w_in, tabs, qg, kg):
    nb, t_all, d = xs.shape
    nt = t_all // TOK
    win = w_in.shape[1]
    tok_major = lambda w: (jax.ShapeDtypeStruct((nb, t_all, w), BF16),
                           pl.BlockSpec((1, TOK, w), lambda b, t: (b, t, 0)))
    feat_major = lambda w: (jax.ShapeDtypeStruct((nb, w, t_all), BF16),
                            pl.BlockSpec((1, w, TOK), lambda b, t: (b, 0, t)))
    outs = [feat_major(256), tok_major(128), feat_major(128),
            feat_major(256), tok_major(256), feat_major(256),
            feat_major(256), tok_major(256), feat_major(256),
            tok_major(1024), tok_major(256)]
    tab_spec = pl.BlockSpec((TOK, 128), lambda b, t: (t, 0))
    return pl.pallas_call(
        functools.partial(_inproj_kernel, n_batch=nb, d_model=d),
        out_shape=[o[0] for o in outs],
        grid=(nb, nt),
        in_specs=[pl.BlockSpec((1, TOK, d), lambda b, t: (b, t, 0)),
                  pl.BlockSpec((SUBLANES, 3 * d), lambda b, t: (0, 0)),
                  pl.BlockSpec((d, win), lambda b, t: (0, 0)),
                  tab_spec, tab_spec, tab_spec, tab_spec,
                  pl.BlockSpec((1, 128), lambda b, t: (0, 0)),
                  pl.BlockSpec((1, 128), lambda b, t: (0, 0))],
        out_specs=[o[1] for o in outs],
        compiler_params=_cparams(2, 56),
        name="in_proj",
    )(xs, mod, w_in, *tabs, qg, kg)


def _pad_rows(x, start, total):
    parts = []
    if start:
        parts.append(jnp.zeros((start, x.shape[1]), x.dtype))
    parts.append(x)
    rest = total - start - x.shape[0]
    if rest:
        parts.append(jnp.zeros((rest, x.shape[1]), x.dtype))
    return jnp.concatenate(parts, axis=0) if len(parts) > 1 else x


def _flash_heads(k_ref, vt_ref, q_pads, v_rows, n_chunks):
    bq = q_pads[0].shape[1]
    n = len(q_pads)

    def body(c, carry):
        off = pl.multiple_of(c * KCHUNK, KCHUNK)
        ks = k_ref[0, pl.ds(off, KCHUNK), :]
        v1 = jnp.concatenate([vt_ref[0, v_rows, pl.ds(off, KCHUNK)],
                              jnp.ones((16, KCHUNK), BF16)], axis=0)
        new = []
        for i in range(n):
            m, acc = carry[2 * i], carry[2 * i + 1]
            s = _dot(ks, q_pads[i])
            m_new = jnp.maximum(m, jnp.max(s, axis=0, keepdims=True))
            alpha = jnp.exp(m - m_new)
            p = jnp.exp(s - m_new).astype(BF16)
            new += [m_new, acc * alpha + _dot(v1, p)]
        return tuple(new)

    init = []
    for _ in range(n):
        init += [jnp.full((1, bq), NEG, F32), jnp.zeros((HEAD + 16, bq), F32)]
    res = lax.fori_loop(0, n_chunks, body, tuple(init))
    return [res[2 * i + 1][0:HEAD] * (1.0 / res[2 * i + 1][HEAD:HEAD + 1]) for i in range(n)]


def _gqa_kernel(qt_ref, k_ref, vt_ref, o_ref, *, t_all):
    def run(n_chunks):
        outs = []
        for h in range(N_HEADS):
            g = h // (N_HEADS // A_KV)
            qh = qt_ref[0, HEAD * h:HEAD * (h + 1), :]
            qpad = _pad_rows(qh, HEAD * g, HEAD * A_KV)
            outs += _flash_heads(k_ref, vt_ref, [qpad], slice(HEAD * g, HEAD * (g + 1)), n_chunks)
        o_ref[0] = jnp.concatenate(outs, axis=0).T.astype(BF16)

    j = pl.program_id(1)
    pl.when(j == 0)(lambda: run(TOK // KCHUNK))
    pl.when(j > 0)(lambda: run(t_all // KCHUNK))


def _gqa_call(qat, ka, vat):
    nb, _, t_all = qat.shape
    return pl.pallas_call(
        functools.partial(_gqa_kernel, t_all=t_all),
        out_shape=jax.ShapeDtypeStruct((nb, t_all, 256), BF16),
        grid=(nb, t_all // TOK),
        in_specs=[pl.BlockSpec((1, 256, TOK), lambda b, j: (b, 0, j)),
                  pl.BlockSpec((1, t_all, 128), lambda b, j: (b, 0, 0)),
                  pl.BlockSpec((1, 128, t_all), lambda b, j: (b, 0, 0))],
        out_specs=pl.BlockSpec((1, TOK, 256), lambda b, j: (b, j, 0)),
        compiler_params=_cparams(2, 32),
        name="gqa_attn",
    )(qat, ka, vat)


def _diff_kernel(qt_ref, k_ref, vt_ref, lq1_ref, lk1_ref, lq2_ref, lk2_ref, g_ref, o_ref, *,
                 t_all, lam_init):
    lam = (jnp.exp(jnp.sum(lq1_ref[...] * lk1_ref[...], axis=1, keepdims=True))
           - jnp.exp(jnp.sum(lq2_ref[...] * lk2_ref[...], axis=1, keepdims=True)) + lam_init)
    half = HEAD // 2

    def run(n_chunks):
        outs = []
        for h in range(N_HEADS):
            q1 = _pad_rows(qt_ref[0, HEAD * h:HEAD * h + half, :], HEAD * h, HEAD * N_HEADS)
            q2 = _pad_rows(qt_ref[0, HEAD * h + half:HEAD * (h + 1), :], HEAD * h + half, HEAD * N_HEADS)
            o1, o2 = _flash_heads(k_ref, vt_ref, [q1, q2], slice(HEAD * h, HEAD * (h + 1)), n_chunks)
            o = o1 - lam * o2
            ms = jnp.mean(o * o, axis=0, keepdims=True)
            outs.append(o * lax.rsqrt(ms + EPS_RMS))
        y = jnp.concatenate(outs, axis=0).T * _tile_lanes(g_ref[...], 2) * (1.0 - lam_init)
        o_ref[0] = y.astype(BF16)

    j = pl.program_id(1)
    pl.when(j == 0)(lambda: run(TOK // KCHUNK))
    pl.when(j > 0)(lambda: run(t_all // KCHUNK))


def _diff_call(qbt, kb, vbt, lq1, lk1, lq2, lk2, g128, lam_init):
    nb, _, t_all = qbt.shape
    vec = pl.BlockSpec((1, HEAD // 2), lambda b, j: (0, 0))
    return pl.pallas_call(
        functools.partial(_diff_kernel, t_all=t_all, lam_init=lam_init),
        out_shape=jax.ShapeDtypeStruct((nb, t_all, 256), BF16),
        grid=(nb, t_all // TOK),
        in_specs=[pl.BlockSpec((1, 256, TOK), lambda b, j: (b, 0, j)),
                  pl.BlockSpec((1, t_all, 256), lambda b, j: (b, 0, 0)),
                  pl.BlockSpec((1, 256, t_all), lambda b, j: (b, 0, 0)),
                  vec, vec, vec, vec,
                  pl.BlockSpec((1, 128), lambda b, j: (0, 0))],
        out_specs=pl.BlockSpec((1, TOK, 256), lambda b, j: (b, j, 0)),
        compiler_params=_cparams(2, 32),
        name="diff_attn",
    )(qbt, kb, vbt, lq1, lk1, lq2, lk2, g128)


NA_WIN = 3


def _na_kernel(qt_ref, k0_ref, k1_ref, k2_ref, kc_ref, v0_ref, v1_ref, v2_ref, vc_ref, bias_ref, o_ref):
    kwin = jnp.concatenate([k0_ref[0], k1_ref[0], k2_ref[0]], axis=0)
    kctx = kc_ref[0]
    ones = jnp.ones((16, TOK), BF16)
    outs = []
    for h in range(N_HEADS):
        rows = slice(HEAD * h, HEAD * (h + 1))
        qpad = _pad_rows(qt_ref[0, rows, :], HEAD * h, HEAD * N_HEADS)
        s_win = _dot(kwin, qpad) + bias_ref[h, 0]
        s_ctx = _dot(kctx, qpad)
        m = jnp.maximum(jnp.max(s_win, axis=0, keepdims=True), jnp.max(s_ctx, axis=0, keepdims=True))
        p_win = jnp.exp(s_win - m).astype(BF16)
        p_ctx = jnp.exp(s_ctx - m).astype(BF16)
        vwin = jnp.concatenate([v0_ref[0, rows, :], v1_ref[0, rows, :], v2_ref[0, rows, :]], axis=1)
        acc = (_dot(jnp.concatenate([vwin, jnp.concatenate([ones] * NA_WIN, axis=1)], axis=0), p_win)
               + _dot(jnp.concatenate([vc_ref[0, rows, :], ones], axis=0), p_ctx))
        outs.append(acc[0:HEAD] * (1.0 / acc[HEAD:HEAD + 1]))
    o_ref[0] = jnp.concatenate(outs, axis=0).T.astype(BF16)


def _na_call(qct, kc, vct, bias):
    nb, _, t_all = qct.shape
    nt = t_all // TOK
    n_lat = nt - 1

    def win(i):
        return lambda b, j: 1 + jnp.clip(j - 2, 0, n_lat - NA_WIN) + i

    def pattern(b, j):
        return jnp.where(j == 0, 3, jnp.where(j == 1, 0, jnp.where(j == nt - 1, 2, 1)))

    k_specs = [pl.BlockSpec((1, TOK, 256), (lambda b, j, f=win(i): (b, f(b, j), 0))) for i in range(NA_WIN)]
    v_specs = [pl.BlockSpec((1, 256, TOK), (lambda b, j, f=win(i): (b, 0, f(b, j)))) for i in range(NA_WIN)]
    return pl.pallas_call(
        _na_kernel,
        out_shape=jax.ShapeDtypeStruct((nb, t_all, 256), BF16),
        grid=(nb, nt),
        in_specs=[pl.BlockSpec((1, 256, TOK), lambda b, j: (b, 0, j))]
                 + k_specs + [pl.BlockSpec((1, TOK, 256), lambda b, j: (b, 0, 0))]
                 + v_specs + [pl.BlockSpec((1, 256, TOK), lambda b, j: (b, 0, 0))]
                 + [pl.BlockSpec((N_HEADS, 1, NA_WIN * TOK, TOK), lambda b, j: (0, pattern(b, j), 0, 0))],
        out_specs=pl.BlockSpec((1, TOK, 256), lambda b, j: (b, j, 0)),
        compiler_params=_cparams(2, 48),
        name="na_attn",
    )(qct, kc, kc, kc, kc, vct, vct, vct, vct, bias)


def _na_bias(rpb, n_rows):
    tile_rows = TOK // GRID_WIDTH
    win_rows = NA_WIN * tile_rows
    n_lat = n_rows // tile_rows

    def tile(r0, w0):
        kr = (w0 + np.arange(win_rows))[:, None, None, None]
        kc = np.arange(GRID_WIDTH)[None, :, None, None]
        qr = (r0 + np.arange(tile_rows))[None, None, :, None]
        qc = np.arange(GRID_WIDTH)[None, None, None, :]
        rs = np.clip(qr - NA_ROWS // 2, 0, n_rows - NA_ROWS)
        cs = np.clip(qc - NA_COLS // 2, 0, GRID_WIDTH - NA_COLS)
        ok = (kr >= rs) & (kr < rs + NA_ROWS) & (kc >= cs) & (kc < cs + NA_COLS)
        dr = np.clip(kr - qr + NA_ROWS - 1, 0, 2 * NA_ROWS - 2)
        dc = np.clip(kc - qc + NA_COLS - 1, 0, 2 * NA_COLS - 2)
        shape = (win_rows * GRID_WIDTH, tile_rows * GRID_WIDTH)
        dr, dc, ok = (np.broadcast_to(a, (win_rows, GRID_WIDTH, tile_rows, GRID_WIDTH)).reshape(shape)
                      for a in (dr, dc, ok))
        return jnp.where(ok[None], rpb[:, dr, dc], NEG)

    last = n_lat - 1
    tiles = [tile(0, 0), tile(tile_rows, 0), tile(last * tile_rows, (last - NA_WIN + 1) * tile_rows),
             jnp.full((rpb.shape[0], win_rows * GRID_WIDTH, TOK), NEG, F32)]
    return jnp.stack(tiles, axis=1).astype(F32)


def _s5_kernel(u_ref, ksum_ref, ws_ref, wo_ref, coef_ref, y_ref, s_scr, h_scr, *, n_chunks, ctx_chunks):
    u = u_ref[0]
    s_scr[...] = _dot(u, ws_ref[0])
    coef = coef_ref[0]
    c1f, c2f, c3f = coef[0:1], coef[1:2], coef[2:3]
    c1b, c2b, c3b = coef[3:4], coef[4:5], coef[5:6]

    def body(k, carry):
        hf, hfs, hb, hbs = carry
        kb = jnp.where(k < ctx_chunks, ctx_chunks - 1 - k, n_chunks - 1 + ctx_chunks - k)
        rf = pl.multiple_of(k * SUBLANES, SUBLANES)
        rb = pl.multiple_of(kb * SUBLANES, SUBLANES)
        h_scr[pl.ds(rf, SUBLANES), 0:128] = hf
        h_scr[pl.ds(rb, SUBLANES), 128:256] = hb
        sf = s_scr[pl.ds(rf, SUBLANES), 0:256]
        sb = s_scr[pl.ds(rb, SUBLANES), 256:512]
        hf_n = hf * c1f + hfs * c2f + sf[:, 0:128]
        hfs_n = hfs * c1f + hf * c3f + sf[:, 128:256]
        hb_n = hb * c1b + hbs * c2b + sb[:, 0:128]
        hbs_n = hbs * c1b + hb * c3b + sb[:, 128:256]
        return hf_n, hfs_n, hb_n, hbs_n

    zero = jnp.zeros((SUBLANES, 128), F32)
    lax.fori_loop(0, n_chunks, body, (zero, zero, zero, zero))
    y_ref[0] = _dot(u, ksum_ref[0]) + _dot(h_scr[...].astype(BF16), wo_ref[0])


def _s5_call(u, ksum, ws, wo, coef, ctx_chunks):
    g, r8, w = u.shape
    n_chunks = r8 // SUBLANES
    grp = lambda shape: pl.BlockSpec((1,) + shape, lambda i: (i, 0, 0))
    return pl.pallas_call(
        functools.partial(_s5_kernel, n_chunks=n_chunks, ctx_chunks=ctx_chunks),
        out_shape=jax.ShapeDtypeStruct((g, r8, w), F32),
        grid=(g,),
        in_specs=[grp((r8, w)), grp((w, w)), grp((w, 512)), grp((w, w)), grp((SUBLANES, 128))],
        out_specs=grp((r8, w)),
        scratch_shapes=[pltpu.VMEM((r8, 512), F32), pltpu.VMEM((r8, 256), F32)],
        compiler_params=_cparams(1, 48),
        name="s5_scan",
    )(u, ksum, ws, wo, coef)


def _s5_matrices(a_re, a_im, log_dt, b_re, b_im, c_re, c_im):
    hp = lax.Precision.HIGHEST
    tc = S5_CHUNK
    dt = jnp.exp(log_dt)[..., None]
    ar, ai = a_re * dt, a_im * dt
    j = jnp.arange(tc + 1, dtype=F32)[:, None, None, None]
    mag = jnp.exp(j * ar[None])
    pr, pi = mag * jnp.cos(j * ai[None]), mag * jnp.sin(j * ai[None])
    nr, ni = pr[1] - 1.0, pi[1]
    den = a_re * a_re + a_im * a_im
    fr = (nr * a_re + ni * a_im) / den
    fi = (ni * a_re - nr * a_im) / den
    bbr = fr[..., None] * b_re - fi[..., None] * b_im
    bbi = fr[..., None] * b_im + fi[..., None] * b_re
    cpr = c_re[None] * pr[:, :, :, None, :] - c_im[None] * pi[:, :, :, None, :]
    cpi = c_re[None] * pi[:, :, :, None, :] + c_im[None] * pr[:, :, :, None, :]
    kern = (jnp.einsum('jdgop,dgpi->jdgoi', cpr, bbr, precision=hp)
            - jnp.einsum('jdgop,dgpi->jdgoi', cpi, bbi, precision=hp))
    s_idx = np.arange(tc)[:, None]
    t_idx = np.arange(tc)[None, :]
    lag_f = np.clip(t_idx - s_idx, 0, tc)
    lag_b = np.clip(s_idx - t_idx, 0, tc)
    kf = jnp.where((t_idx >= s_idx)[:, :, None, None, None], kern[lag_f, 0], 0.0)
    kb = jnp.where((s_idx >= t_idx)[:, :, None, None, None], kern[lag_b, 1], 0.0)
    ksum = jnp.transpose(kf + kb, (2, 0, 4, 1, 3))
    g_, p_ = a_re.shape[1], a_re.shape[2]
    ch = b_re.shape[-1]
    ksum = ksum.reshape(g_, tc * ch, tc * ch)

    def state_in(d, powers):
        wr = pr[powers, d][..., None] * bbr[d][None] - pi[powers, d][..., None] * bbi[d][None]
        wi = pr[powers, d][..., None] * bbi[d][None] + pi[powers, d][..., None] * bbr[d][None]
        wr = jnp.transpose(wr, (1, 0, 3, 2)).reshape(g_, tc * ch, p_)
        wi = jnp.transpose(wi, (1, 0, 3, 2)).reshape(g_, tc * ch, p_)
        return jnp.concatenate([wr, wi, wi, wr], axis=-1)
    ws = jnp.concatenate([state_in(0, tc - 1 - np.arange(tc)), state_in(1, np.arange(tc))], axis=-1)

    def state_out(d, powers):
        gr = cpr[powers, d]
        gi = cpi[powers, d]
        top = jnp.transpose(gr, (1, 3, 0, 2)).reshape(g_, p_, tc * ch)
        bot = jnp.transpose(-gi, (1, 3, 0, 2)).reshape(g_, p_, tc * ch)
        return jnp.concatenate([top, bot], axis=1)
    wo = jnp.concatenate([state_out(0, 1 + np.arange(tc)), state_out(1, tc - np.arange(tc))], axis=1)

    acr, aci = pr[tc], pi[tc]
    rows = []
    for d in range(2):
        rows += [jnp.concatenate([acr[d], acr[d]], -1), jnp.concatenate([-aci[d], aci[d]], -1),
                 jnp.concatenate([aci[d], -aci[d]], -1)]
    rows += [jnp.zeros_like(rows[0])] * 2
    coef = jnp.stack(rows, axis=1)
    return ksum.astype(BF16), ws.astype(BF16), wo.astype(BF16), coef.astype(F32)


def _s5_mixer(du, mats, ctx_len):
    nb, t_all, w = du.shape
    ksum, ws, wo, coef = mats
    g = ksum.shape[0]
    ch = w // g
    nch = t_all // S5_CHUNK
    u = du.reshape(nb, nch, S5_CHUNK, g, ch)
    u = jnp.transpose(u, (3, 1, 0, 2, 4))
    u = jnp.pad(u, ((0, 0), (0, 0), (0, SUBLANES - nb), (0, 0), (0, 0)))
    u = u.reshape(g, nch * SUBLANES, S5_CHUNK * ch)
    y = _s5_call(u, ksum, ws, wo, coef, ctx_len // S5_CHUNK)
    y = y.reshape(g, nch, SUBLANES, S5_CHUNK, ch)[:, :, :nb]
    return jnp.transpose(y, (2, 1, 3, 0, 4)).reshape(nb, t_all, w)


def _outproj_kernel(ya_ref, yb_ref, yc_ref, ys_ref, du_ref, gate_ref, x_ref, mod_ref, dskip_ref, wglu_ref,
                    wout_ref, lng_ref, lnb_ref, o_ref, *, n_batch, d_model, alpha):
    b = pl.program_id(0)
    t = pl.program_id(1)
    row = jnp.where(t == 0, n_batch, b)
    gate_vec = mod_ref[pl.ds(row, 1), 2 * d_model:3 * d_model]

    yd = ys_ref[0] + dskip_ref[...] * du_ref[0].astype(F32)
    zz = _dot(jax.nn.gelu(yd).astype(BF16), wglu_ref[...])
    yd = zz[:, 0:256] * jax.nn.sigmoid(zz[:, 256:512])
    gates = gate_ref[0]
    parts = [ya_ref[0], yb_ref[0], yc_ref[0]]
    cat = [parts[i] * gates[:, 256 * i:256 * (i + 1)] for i in range(3)]
    cat.append((yd * gates[:, 768:1024].astype(F32)).astype(BF16))
    y = _dot(jnp.concatenate(cat, axis=1), wout_ref[...])
    v = alpha * x_ref[0] + gate_vec * y
    mu = jnp.mean(v, axis=1, keepdims=True)
    vc = v - mu
    var = jnp.mean(vc * vc, axis=1, keepdims=True)
    o_ref[0] = vc * lax.rsqrt(var + EPS_LN) * lng_ref[...] + lnb_ref[...]


def _outproj_call(ya, yb, yc, ys, du, gates, xs, mod, dskip, wglu, wout, lng, lnb, alpha):
    nb, t_all, d = xs.shape
    tm = lambda w: pl.BlockSpec((1, TOK, w), lambda b, t: (b, t, 0))
    full = lambda a: pl.BlockSpec(a.shape, lambda b, t: (0,) * a.ndim)
    return pl.pallas_call(
        functools.partial(_outproj_kernel, n_batch=nb, d_model=d, alpha=alpha),
        out_shape=jax.ShapeDtypeStruct((nb, t_all, d), F32),
        grid=(nb, t_all // TOK),
        in_specs=[tm(256), tm(256), tm(256), tm(256), tm(256), tm(1024), tm(d),
                  full(mod), full(dskip), full(wglu), full(wout), full(lng), full(lnb)],
        out_specs=tm(d),
        compiler_params=_cparams(2, 40),
        name="out_proj",
    )(ya, yb, yc, ys, du, gates, xs, mod, dskip, wglu, wout, lng, lnb)


def _rope_tables(seq, ctx_len):
    t = jnp.arange(seq, dtype=jnp.int32)
    row = (t // GRID_WIDTH).astype(F32)
    col = (t % GRID_WIDTH).astype(F32)

    def tab(half):
        inv = ROPE_THETA ** (-jnp.arange(half, dtype=F32) / half)
        ar, ac = row[:, None] * inv, col[:, None] * inv
        cos = jnp.concatenate([jnp.cos(ar), jnp.cos(ar), jnp.cos(ac), jnp.cos(ac)], axis=1)
        sin = jnp.concatenate([-jnp.sin(ar), jnp.sin(ar), -jnp.sin(ac), jnp.sin(ac)], axis=1)
        cos = jnp.concatenate([jnp.ones((ctx_len, 4 * half), F32), cos], axis=0)
        sin = jnp.concatenate([jnp.zeros((ctx_len, 4 * half), F32), sin], axis=0)
        reps = 128 // (4 * half)
        return jnp.tile(cos, (1, reps)), jnp.tile(sin, (1, reps))

    cosa, sina = tab(HEAD // 4)
    cosb, sinb = tab(HEAD // 8)
    return cosa, sina, cosb, sinb


def kernel(x, c, ctx, c_ctx, w_ada, b_ada, w_in, w_out, ln_g, ln_b, qn_g, kn_g, lam_q1, lam_k1, lam_q2,
           lam_k2, subln_g, na_rpb, s5_a_re, s5_a_im, s5_log_dt, s5_b_re, s5_b_im, s5_c_re, s5_c_im, s5_d,
           w_glu):
    nb, seq, d = x.shape
    ctx_len = ctx.shape[1]
    depth = w_ada.shape[0]
    assert ctx_len == TOK and seq % TOK == 0 and nb < SUBLANES
    alpha = (2 * depth) ** 0.25

    cc = jnp.concatenate([c, c_ctx[None], jnp.zeros((SUBLANES - nb - 1, d), F32)], axis=0)
    mods = _ada_call(cc, w_ada, b_ada)
    tabs = _rope_tables(seq, ctx_len)
    xs = jnp.concatenate([ctx, x], axis=1)
    n_rows = seq // GRID_WIDTH

    for l in range(depth):
        lam_init = 0.8 - 0.6 * math.exp(-0.3 * l)
        qg = jnp.tile(qn_g[l][None], (1, 2))
        kg = jnp.tile(kn_g[l][None], (1, 2))
        (qat, ka, vat, qbt, kb, vbt, qct, kc, vct, gates, du) = _inproj_call(
            xs, mods[l], w_in[l].astype(BF16), tabs, qg, kg)
        ya = _gqa_call(qat, ka, vat)
        yb = _diff_call(qbt, kb, vbt, lam_q1[l][None], lam_k1[l][None], lam_q2[l][None], lam_k2[l][None],
                        jnp.tile(subln_g[l][None], (1, 2)), lam_init)
        yc = _na_call(qct, kc, vct, _na_bias(na_rpb[l], n_rows))
        mats = _s5_matrices(s5_a_re[l], s5_a_im[l], s5_log_dt[l], s5_b_re[l], s5_b_im[l], s5_c_re[l],
                            s5_c_im[l])
        ys = _s5_mixer(du, mats, ctx_len)
        xs = _outproj_call(ya, yb, yc, ys, du, gates, xs, mods[l], s5_d[l][None], w_glu[l].astype(BF16),
                           w_out[l].astype(BF16), ln_g[l][None], ln_b[l][None], alpha)
    return xs[:, ctx_len:]
```

```python
import functools
import math

import jax
import jax.numpy as jnp
import numpy as np
from jax import lax
from jax.experimental import pallas as pl
from jax.experimental.pallas import tpu as pltpu

F32 = jnp.float32
BF16 = jnp.bfloat16

GRID_WIDTH = 64
HEAD = 64
N_HEADS = 4
A_KV = 2
NA_ROWS = 8
NA_COLS = 16
ROPE_THETA = 10000.0
EPS_RMS = 1e-6
EPS_LN = 1e-5

TOK = 256
KCHUNK = 512
S5_CHUNK = 16
SUBLANES = 8
NEG = -1e30
LOG2E = math.log2(math.e)
MIB = 1024 * 1024


def _cparams(n_axes, vmem_mib):
    return pltpu.CompilerParams(dimension_semantics=("parallel",) * n_axes,
                                vmem_limit_bytes=vmem_mib * MIB)


def _dot(a, b):
    return jnp.dot(a, b, preferred_element_type=F32)


def _ada_kernel(c_ref, w_ref, b_ref, o_ref):
    c = c_ref[...]
    s = c * jax.nn.sigmoid(c)
    o_ref[0] = jnp.dot(s, w_ref[0], preferred_element_type=F32,
                       precision=lax.Precision.HIGHEST) + b_ref[0]


def _ada_call(cc, w_ada, b_ada):
    depth, d, d3 = w_ada.shape
    nblk = 3
    bw = d3 // nblk
    return pl.pallas_call(
        _ada_kernel,
        out_shape=jax.ShapeDtypeStruct((depth, SUBLANES, d3), F32),
        grid=(depth, nblk),
        in_specs=[pl.BlockSpec((SUBLANES, d), lambda l, n: (0, 0)),
                  pl.BlockSpec((1, d, bw), lambda l, n: (l, 0, n)),
                  pl.BlockSpec((1, 1, bw), lambda l, n: (l, 0, n))],
        out_specs=pl.BlockSpec((1, SUBLANES, bw), lambda l, n: (l, 0, n)),
        compiler_params=_cparams(2, 32),
        name="ada_mod",
    )(cc, w_ada, b_ada.reshape(depth, 1, d3))


def _swap_halves(x, half):
    out = []
    lane = lax.broadcasted_iota(jnp.int32, (1, 128), 1)
    first = (lane % (2 * half)) < half
    for s in range(x.shape[1] // 128):
        xs = x[:, 128 * s:128 * (s + 1)]
        fwd = pltpu.roll(xs, 128 - half, axis=1)
        bwd = pltpu.roll(xs, half, axis=1)
        out.append(jnp.where(first, fwd, bwd))
    return out[0] if len(out) == 1 else jnp.concatenate(out, axis=1)


def _tile_lanes(t, reps):
    return t if reps == 1 else jnp.concatenate([t] * reps, axis=1)


def _head_meansq(x):
    w = x.shape[1]
    r = lax.broadcasted_iota(jnp.int32, (w, w), 0) // HEAD
    c = lax.broadcasted_iota(jnp.int32, (w, w), 1) // HEAD
    bd = jnp.where(r == c, 1.0, 0.0).astype(BF16)
    xx = x * x
    hi = xx.astype(BF16)
    lo = (xx - hi.astype(F32)).astype(BF16)
    return (_dot(hi, bd) + _dot(lo, bd)) * (1.0 / HEAD)


def _inproj_kernel(x_ref, ctx_ref, mod_ref, w_ref, cosa_ref, sina_ref, cosb_ref, sinb_ref, qg_ref, kg_ref,
                   qat_ref, ka_ref, vat_ref, qbt_ref, kb_ref, vbt_ref, qct_ref, kc_ref, vct_ref,
                   gate_ref, du_ref, *, n_batch, d_model):
    b = pl.program_id(0)
    t = pl.program_id(1)
    row = jnp.where(t == 0, n_batch, b)
    mod = mod_ref[pl.ds(row, 1), :]
    shift = mod[:, :d_model]
    scale = mod[:, d_model:2 * d_model]
    xin = jnp.where(t == 0, ctx_ref[0], x_ref[0])
    h = (xin * (1.0 + scale) + shift).astype(BF16)
    z = _dot(h, w_ref[...])

    cosa, sina = cosa_ref[...], sina_ref[...]
    cosb, sinb = cosb_ref[...], sinb_ref[...]

    aq = z[:, 0:256]
    aq = aq * lax.rsqrt(_head_meansq(aq) + EPS_RMS) * _tile_lanes(qg_ref[...], 2)
    aq = aq * _tile_lanes(cosa, 2) + _swap_halves(aq, 16) * _tile_lanes(sina, 2)
    qat_ref[0] = (aq * (HEAD ** -0.5 * LOG2E)).T.astype(BF16)
    ak = z[:, 256:384]
    ak = ak * lax.rsqrt(_head_meansq(ak) + EPS_RMS) * kg_ref[...]
    ak = ak * cosa + _swap_halves(ak, 16) * sina
    ka_ref[0] = ak.astype(BF16)
    vat_ref[0] = z[:, 384:512].T.astype(BF16)

    bq = z[:, 768:1024]
    bq = bq * _tile_lanes(cosb, 2) + _swap_halves(bq, 8) * _tile_lanes(sinb, 2)
    qbt_ref[0] = (bq * ((HEAD // 2) ** -0.5 * LOG2E)).T.astype(BF16)
    bk = z[:, 1024:1280]
    bk = bk * _tile_lanes(cosb, 2) + _swap_halves(bk, 8) * _tile_lanes(sinb, 2)
    kb_ref[0] = bk.astype(BF16)
    vbt_ref[0] = z[:, 1280:1536].T.astype(BF16)

    qct_ref[0] = (z[:, 1792:2048] * (HEAD ** -0.5 * LOG2E)).T.astype(BF16)
    kc_ref[0] = z[:, 2048:2304].astype(BF16)
    vct_ref[0] = z[:, 2304:2560].T.astype(BF16)

    du_ref[0] = z[:, 2816:3072].astype(BF16)
    for i, c0 in enumerate((512, 1536, 2560, 3072)):
        g = z[:, c0:c0 + 256]
        gate_ref[0, :, 256 * i:256 * (i + 1)] = (g * jax.nn.sigmoid(g)).astype(BF16)


def _inproj_call(x, ctx, mod, w_in, tabs, qg, kg):
    nb, seq, d = x.shape
    t_all = seq + ctx.shape[1]
    nt = t_all // TOK
    win = w_in.shape[1]
    tok_major = lambda w: (jax.ShapeDtypeStruct((nb, t_all, w), BF16),
                           pl.BlockSpec((1, TOK, w), lambda b, t: (b, t, 0)))
    feat_major = lambda w: (jax.ShapeDtypeStruct((nb, w, t_all), BF16),
                            pl.BlockSpec((1, w, TOK), lambda b, t: (b, 0, t)))
    outs = [feat_major(256), tok_major(128), feat_major(128),
            feat_major(256), tok_major(256), feat_major(256),
            feat_major(256), tok_major(256), feat_major(256),
            tok_major(1024), tok_major(256)]
    tab_spec = pl.BlockSpec((TOK, 128), lambda b, t: (t, 0))
    return pl.pallas_call(
        functools.partial(_inproj_kernel, n_batch=nb, d_model=d),
        out_shape=[o[0] for o in outs],
        grid=(nb, nt),
        in_specs=[pl.BlockSpec((1, TOK, d), lambda b, t: (b, jnp.maximum(t - 1, 0), 0)),
                  pl.BlockSpec((1, TOK, d), lambda b, t: (b, 0, 0)),
                  pl.BlockSpec((SUBLANES, 3 * d), lambda b, t: (0, 0)),
                  pl.BlockSpec((d, win), lambda b, t: (0, 0)),
                  tab_spec, tab_spec, tab_spec, tab_spec,
                  pl.BlockSpec((1, 128), lambda b, t: (0, 0)),
                  pl.BlockSpec((1, 128), lambda b, t: (0, 0))],
        out_specs=[o[1] for o in outs],
        compiler_params=_cparams(2, 56),
        name="in_proj",
    )(x, ctx, mod, w_in, *tabs, qg, kg)


def _pad_rows(x, start, total):
    parts = []
    if start:
        parts.append(jnp.zeros((start, x.shape[1]), x.dtype))
    parts.append(x)
    rest = total - start - x.shape[0]
    if rest:
        parts.append(jnp.zeros((rest, x.shape[1]), x.dtype))
    return jnp.concatenate(parts, axis=0) if len(parts) > 1 else x


def _colmax(parts):
    parts = list(parts)
    while len(parts) > 1:
        parts = [jnp.maximum(parts[2 * i], parts[2 * i + 1]) for i in range(len(parts) // 2)] + parts[len(parts) // 2 * 2:]
    y = parts[0]
    while y.shape[0] > SUBLANES:
        half = y.shape[0] // 2
        y = jnp.maximum(y[:half], y[half:])
    return jnp.max(y, axis=0, keepdims=True)


def _flash_maps(k_ref, vt_ref, q_pads, v_rows, latent, t_all, slots):
    bq = q_pads[0].shape[1]
    n = len(q_pads)

    def scores(off, size, slot=None):
        nsub = size // TOK
        ks = [k_ref[0, pl.ds(off + TOK * r, TOK), :] for r in range(nsub)]
        s = [[_dot(ks[r], q_pads[i]) for r in range(nsub)] for i in range(n)]
        cmax = tuple(_colmax(s[i]) for i in range(n))
        if slot is None:
            return s, cmax
        for i in range(n):
            for r in range(nsub):
                slot[i, TOK * r:TOK * (r + 1), :] = s[i][r]
        return cmax

    def accumulate(stats, s, cmax, off, size):
        nsub = size // TOK
        ones = jnp.ones((16, size), BF16)
        v1 = {}
        new = []
        for i in range(n):
            key = (v_rows[i].start, v_rows[i].stop)
            if key not in v1:
                v1[key] = jnp.concatenate([vt_ref[0, v_rows[i], pl.ds(off, size)], ones], axis=0)
            m_new = jnp.maximum(stats[2 * i], cmax[i])
            p = []
            for r in range(nsub):
                blk = s[i][r] if isinstance(s, list) else s[i, TOK * r:TOK * (r + 1), :]
                p.append(jnp.exp2(blk - m_new).astype(BF16))
            p = p[0] if nsub == 1 else jnp.concatenate(p, axis=0)
            acc = stats[2 * i + 1] * jnp.exp2(stats[2 * i] - m_new) + _dot(v1[key], p)
            new += [m_new, acc]
        return tuple(new)

    stats = []
    for _ in range(n):
        stats += [jnp.full((1, bq), NEG, F32), jnp.zeros((HEAD + 16, bq), F32)]
    stats = tuple(stats)
    s_ctx, c_ctx = scores(0, TOK)
    if not latent:
        stats = accumulate(stats, s_ctx, c_ctx, 0, TOK)
    else:
        s0, s1 = slots
        n_lat = (t_all - TOK) // KCHUNK
        assert n_lat % 2 == 0
        lat_off = lambda c: pl.multiple_of(TOK + c * KCHUNK, TOK)
        c0 = scores(TOK, KCHUNK, s0)
        stats = accumulate(stats, s_ctx, c_ctx, 0, TOK)

        def body(k, carry):
            stats, c0 = carry
            c1 = scores(lat_off(2 * k + 1), KCHUNK, s1)
            stats = accumulate(stats, s0, c0, lat_off(2 * k), KCHUNK)
            c0 = scores(lat_off(2 * k + 2), KCHUNK, s0)
            stats = accumulate(stats, s1, c1, lat_off(2 * k + 1), KCHUNK)
            return stats, c0

        stats, c0 = lax.fori_loop(0, n_lat // 2 - 1, body, (stats, c0))
        c1 = scores(lat_off(n_lat - 1), KCHUNK, s1)
        stats = accumulate(stats, s0, c0, lat_off(n_lat - 2), KCHUNK)
        stats = accumulate(stats, s1, c1, lat_off(n_lat - 1), KCHUNK)
    return [stats[2 * i + 1][0:HEAD] * (1.0 / stats[2 * i + 1][HEAD:HEAD + 1]) for i in range(n)]


def _gqa_kernel(qt_ref, k_ref, vt_ref, o_ref, s0_ref, s1_ref, *, t_all):
    def run(latent):
        rep = N_HEADS // A_KV
        q_pads = [_pad_rows(qt_ref[0, HEAD * h:HEAD * (h + 1), :], HEAD * (h // rep), HEAD * A_KV)
                  for h in range(N_HEADS)]
        v_rows = [slice(HEAD * (h // rep), HEAD * (h // rep + 1)) for h in range(N_HEADS)]
        outs = _flash_maps(k_ref, vt_ref, q_pads, v_rows, latent, t_all, (s0_ref, s1_ref))
        o_ref[0] = jnp.concatenate(outs, axis=0).T.astype(BF16)

    j = pl.program_id(1)
    pl.when(j == 0)(lambda: run(False))
    pl.when(j > 0)(lambda: run(True))


def _gqa_call(qat, ka, vat):
    nb, _, t_all = qat.shape
    return pl.pallas_call(
        functools.partial(_gqa_kernel, t_all=t_all),
        out_shape=jax.ShapeDtypeStruct((nb, t_all, 256), BF16),
        grid=(nb, t_all // TOK),
        in_specs=[pl.BlockSpec((1, 256, TOK), lambda b, j: (b, 0, j)),
                  pl.BlockSpec((1, t_all, 128), lambda b, j: (b, 0, 0)),
                  pl.BlockSpec((1, 128, t_all), lambda b, j: (b, 0, 0))],
        out_specs=pl.BlockSpec((1, TOK, 256), lambda b, j: (b, j, 0)),
        scratch_shapes=[pltpu.VMEM((N_HEADS, KCHUNK, TOK), F32)] * 2,
        compiler_params=_cparams(2, 32),
        name="gqa_attn",
    )(qat, ka, vat)


def _diff_kernel(qt_ref, k_ref, vt_ref, lq1_ref, lk1_ref, lq2_ref, lk2_ref, g_ref, o_ref, s0_ref, s1_ref,
                 *, t_all, lam_init):
    lam = (jnp.exp(jnp.sum(lq1_ref[...] * lk1_ref[...], axis=1, keepdims=True))
           - jnp.exp(jnp.sum(lq2_ref[...] * lk2_ref[...], axis=1, keepdims=True)) + lam_init)
    half = HEAD // 2

    def run(latent):
        q_pads, v_rows = [], []
        for h in range(N_HEADS):
            q_pads.append(_pad_rows(qt_ref[0, HEAD * h:HEAD * h + half, :], HEAD * h, HEAD * N_HEADS))
            q_pads.append(_pad_rows(qt_ref[0, HEAD * h + half:HEAD * (h + 1), :], HEAD * h + half,
                                    HEAD * N_HEADS))
            v_rows += [slice(HEAD * h, HEAD * (h + 1))] * 2
        res = _flash_maps(k_ref, vt_ref, q_pads, v_rows, latent, t_all, (s0_ref, s1_ref))
        outs = []
        for h in range(N_HEADS):
            o = res[2 * h] - lam * res[2 * h + 1]
            ms = jnp.mean(o * o, axis=0, keepdims=True)
            outs.append(o * lax.rsqrt(ms + EPS_RMS))
        y = jnp.concatenate(outs, axis=0).T * _tile_lanes(g_ref[...], 2) * (1.0 - lam_init)
        o_ref[0] = y.astype(BF16)

    j = pl.program_id(1)
    pl.when(j == 0)(lambda: run(False))
    pl.when(j > 0)(lambda: run(True))


def _diff_call(qbt, kb, vbt, lq1, lk1, lq2, lk2, g128, lam_init):
    nb, _, t_all = qbt.shape
    vec = pl.BlockSpec((1, HEAD // 2), lambda b, j: (0, 0))
    return pl.pallas_call(
        functools.partial(_diff_kernel, t_all=t_all, lam_init=lam_init),
        out_shape=jax.ShapeDtypeStruct((nb, t_all, 256), BF16),
        grid=(nb, t_all // TOK),
        in_specs=[pl.BlockSpec((1, 256, TOK), lambda b, j: (b, 0, j)),
                  pl.BlockSpec((1, t_all, 256), lambda b, j: (b, 0, 0)),
                  pl.BlockSpec((1, 256, t_all), lambda b, j: (b, 0, 0)),
                  vec, vec, vec, vec,
                  pl.BlockSpec((1, 128), lambda b, j: (0, 0))],
        out_specs=pl.BlockSpec((1, TOK, 256), lambda b, j: (b, j, 0)),
        scratch_shapes=[pltpu.VMEM((2 * N_HEADS, KCHUNK, TOK), F32)] * 2,
        compiler_params=_cparams(2, 40),
        name="diff_attn",
    )(qbt, kb, vbt, lq1, lk1, lq2, lk2, g128)


NA_WIN = 3


def _na_kernel(qt_ref, k0_ref, k1_ref, k2_ref, kc_ref, v0_ref, v1_ref, v2_ref, vc_ref, bias_ref, o_ref):
    kwin = jnp.concatenate([k0_ref[0], k1_ref[0], k2_ref[0]], axis=0)
    kctx = kc_ref[0]
    ones = jnp.ones((16, TOK), BF16)
    heads = range(N_HEADS)
    rows = [slice(HEAD * h, HEAD * (h + 1)) for h in heads]
    qpad = [_pad_rows(qt_ref[0, rows[h], :], HEAD * h, HEAD * N_HEADS) for h in heads]
    s_win = [_dot(kwin, qpad[h]) + bias_ref[h, 0] for h in heads]
    s_ctx = [_dot(kctx, qpad[h]) for h in heads]
    m = [_colmax([s_win[h][TOK * r:TOK * (r + 1)] for r in range(NA_WIN)] + [s_ctx[h]]) for h in heads]
    outs = []
    for h in heads:
        p_win = jnp.exp2(s_win[h] - m[h]).astype(BF16)
        p_ctx = jnp.exp2(s_ctx[h] - m[h]).astype(BF16)
        vwin = jnp.concatenate([v0_ref[0, rows[h], :], v1_ref[0, rows[h], :], v2_ref[0, rows[h], :]], axis=1)
        acc = (_dot(jnp.concatenate([vwin, jnp.concatenate([ones] * NA_WIN, axis=1)], axis=0), p_win)
               + _dot(jnp.concatenate([vc_ref[0, rows[h], :], ones], axis=0), p_ctx))
        outs.append(acc[0:HEAD] * (1.0 / acc[HEAD:HEAD + 1]))
    o_ref[0] = jnp.concatenate(outs, axis=0).T.astype(BF16)


def _na_call(qct, kc, vct, bias):
    nb, _, t_all = qct.shape
    nt = t_all // TOK
    n_lat = nt - 1

    def win(i):
        return lambda b, j: 1 + jnp.clip(j - 2, 0, n_lat - NA_WIN) + i

    def pattern(b, j):
        return jnp.where(j == 0, 3, jnp.where(j == 1, 0, jnp.where(j == nt - 1, 2, 1)))

    k_specs = [pl.BlockSpec((1, TOK, 256), (lambda b, j, f=win(i): (b, f(b, j), 0))) for i in range(NA_WIN)]
    v_specs = [pl.BlockSpec((1, 256, TOK), (lambda b, j, f=win(i): (b, 0, f(b, j)))) for i in range(NA_WIN)]
    return pl.pallas_call(
        _na_kernel,
        out_shape=jax.ShapeDtypeStruct((nb, t_all, 256), BF16),
        grid=(nb, nt),
        in_specs=[pl.BlockSpec((1, 256, TOK), lambda b, j: (b, 0, j))]
                 + k_specs + [pl.BlockSpec((1, TOK, 256), lambda b, j: (b, 0, 0))]
                 + v_specs + [pl.BlockSpec((1, 256, TOK), lambda b, j: (b, 0, 0))]
                 + [pl.BlockSpec((N_HEADS, 1, NA_WIN * TOK, TOK), lambda b, j: (0, pattern(b, j), 0, 0))],
        out_specs=pl.BlockSpec((1, TOK, 256), lambda b, j: (b, j, 0)),
        compiler_params=_cparams(2, 48),
        name="na_attn",
    )(qct, kc, kc, kc, kc, vct, vct, vct, vct, bias)


def _na_bias(rpb, n_rows):
    tile_rows = TOK // GRID_WIDTH
    win_rows = NA_WIN * tile_rows
    n_lat = n_rows // tile_rows

    kc = np.arange(GRID_WIDTH)[:, None]
    qc = np.arange(GRID_WIDTH)[None, :]
    cs = np.clip(qc - NA_COLS // 2, 0, GRID_WIDTH - NA_COLS)
    col_ok = (kc >= cs) & (kc < cs + NA_COLS)
    dc = kc - qc + NA_COLS - 1
    e_col = ((dc[None] == np.arange(2 * NA_COLS - 1)[:, None, None]) & col_ok[None]).astype(np.float32)

    def tile(r0, w0):
        kr = (w0 + np.arange(win_rows))[:, None]
        qr = (r0 + np.arange(tile_rows))[None, :]
        rs = np.clip(qr - NA_ROWS // 2, 0, n_rows - NA_ROWS)
        row_ok = (kr >= rs) & (kr < rs + NA_ROWS)
        dr = kr - qr + NA_ROWS - 1
        e_row = ((dr[None] == np.arange(2 * NA_ROWS - 1)[:, None, None]) & row_ok[None]).astype(np.float32)
        val = jnp.einsum('hrd,rab,dkq->hakbq', rpb, e_row, e_col, precision=lax.Precision.HIGHEST)
        ok = row_ok[:, None, :, None] & col_ok[None, :, None, :]
        shape = (win_rows * GRID_WIDTH, tile_rows * GRID_WIDTH)
        return jnp.where(ok.reshape(shape)[None], val.reshape((rpb.shape[0],) + shape) * LOG2E, NEG)

    last = n_lat - 1
    tiles = [tile(0, 0), tile(tile_rows, 0), tile(last * tile_rows, (last - NA_WIN + 1) * tile_rows),
             jnp.full((rpb.shape[0], win_rows * GRID_WIDTH, TOK), NEG, F32)]
    return jnp.stack(tiles, axis=1).astype(F32)


def _s5_kernel(u_ref, kt_ref, ws_ref, wot_ref, coef_ref, y_ref, s_scr, ssw_scr, hf_scr, hb_scr, *,
               n_chunks, ctx_chunks):
    tc = S5_CHUNK
    half = SUBLANES // 2
    ut = jnp.concatenate([u_ref[i, 0] for i in range(tc)], axis=0)
    u = ut.astype(F32).T.astype(BF16)
    s_all = _dot(u, ws_ref[0])
    for b in range(half):
        rows = slice(b * n_chunks, (b + 1) * n_chunks)
        s_scr[pl.ds(b, n_chunks, stride=SUBLANES), :] = s_all[rows, 0:128]
        ssw_scr[pl.ds(b, n_chunks, stride=SUBLANES), :] = s_all[rows, 128:256]
        s_scr[pl.ds(half + b, n_chunks, stride=SUBLANES), :] = s_all[rows, 256:384]
        ssw_scr[pl.ds(half + b, n_chunks, stride=SUBLANES), :] = s_all[rows, 384:512]
    c1, c2, c3 = coef_ref[0, 0], coef_ref[0, 1], coef_ref[0, 2]
    is_fwd = lax.broadcasted_iota(jnp.int32, (SUBLANES, 128), 0) < half

    def body(k, carry):
        h, hs = carry
        kb = jnp.where(k < ctx_chunks, ctx_chunks - 1 - k, n_chunks - 1 + ctx_chunks - k)
        rf = pl.multiple_of(k * SUBLANES, SUBLANES)
        rb = pl.multiple_of(kb * SUBLANES, SUBLANES)
        hf_scr[pl.ds(rf, SUBLANES), :] = h
        hb_scr[pl.ds(rb, SUBLANES), :] = h
        sk = jnp.where(is_fwd, s_scr[pl.ds(rf, SUBLANES), :], s_scr[pl.ds(rb, SUBLANES), :])
        ssk = jnp.where(is_fwd, ssw_scr[pl.ds(rf, SUBLANES), :], ssw_scr[pl.ds(rb, SUBLANES), :])
        return h * c1 + hs * c2 + sk, hs * c1 + h * c3 + ssk

    zero = jnp.zeros((SUBLANES, 128), F32)
    lax.fori_loop(0, n_chunks, body, (zero, zero), unroll=4)
    pad = jnp.zeros((u.shape[0] - half * n_chunks, 256), F32)
    hcat = jnp.concatenate(
        [jnp.concatenate([hf_scr[pl.ds(b, n_chunks, stride=SUBLANES), :],
                          hb_scr[pl.ds(half + b, n_chunks, stride=SUBLANES), :]], axis=1)
         for b in range(half)] + ([pad] if pad.shape[0] else []), axis=0)
    nt = (((1,), (1,)), ((), ()))
    y = _dot(kt_ref[0], ut) + lax.dot_general(wot_ref[0], hcat.astype(BF16), nt, preferred_element_type=F32)
    for t in range(tc):
        y_ref[t, 0] = y[tc * t:tc * (t + 1)]


def _s5_call(ut, kt, ws, wot, coef, n_chunks, ctx_chunks):
    tc, g, ch, r = ut.shape
    w = tc * ch
    grp = lambda *shape: pl.BlockSpec((1,) + shape, lambda i: (i,) + (0,) * len(shape))
    io = pl.BlockSpec((tc, 1, ch, r), lambda i: (0, i, 0, 0))
    tile_rows = n_chunks * SUBLANES
    return pl.pallas_call(
        functools.partial(_s5_kernel, n_chunks=n_chunks, ctx_chunks=ctx_chunks),
        out_shape=jax.ShapeDtypeStruct((tc, g, ch, r), F32),
        grid=(g,),
        in_specs=[io, grp(w, w), grp(w, 512), grp(w, w), grp(3, SUBLANES, 128)],
        out_specs=io,
        scratch_shapes=[pltpu.VMEM((tile_rows, 128), F32)] * 4,
        compiler_params=_cparams(1, 48),
        name="s5_scan",
    )(ut, kt, ws, wot, coef)


def _s5_matrices(a_re, a_im, log_dt, b_re, b_im, c_re, c_im):
    hp = lax.Precision.HIGHEST
    tc = S5_CHUNK
    dt = jnp.exp(log_dt)[..., None]
    ar, ai = a_re * dt, a_im * dt
    j = jnp.arange(tc + 1, dtype=F32)[:, None, None, None]
    mag = jnp.exp(j * ar[None])
    pr, pi = mag * jnp.cos(j * ai[None]), mag * jnp.sin(j * ai[None])
    nr, ni = pr[1] - 1.0, pi[1]
    den = a_re * a_re + a_im * a_im
    fr = (nr * a_re + ni * a_im) / den
    fi = (ni * a_re - nr * a_im) / den
    bbr = fr[..., None] * b_re - fi[..., None] * b_im
    bbi = fr[..., None] * b_im + fi[..., None] * b_re
    cpr = c_re[None] * pr[:, :, :, None, :] - c_im[None] * pi[:, :, :, None, :]
    cpi = c_re[None] * pi[:, :, :, None, :] + c_im[None] * pr[:, :, :, None, :]
    kern = (jnp.einsum('jdgop,dgpi->jdgoi', cpr, bbr, precision=hp)
            - jnp.einsum('jdgop,dgpi->jdgoi', cpi, bbi, precision=hp))
    s_idx = np.arange(tc)[:, None]
    t_idx = np.arange(tc)[None, :]
    lags = np.arange(tc + 1)[:, None, None]
    e_f = ((t_idx - s_idx)[None] == lags).astype(np.float32)
    e_b = ((s_idx - t_idx)[None] == lags).astype(np.float32)
    kt = (jnp.einsum('jst,jgoi->gtosi', e_f, kern[:, 0], precision=hp)
          + jnp.einsum('jst,jgoi->gtosi', e_b, kern[:, 1], precision=hp))
    g_, p_ = a_re.shape[1], a_re.shape[2]
    ch = b_re.shape[-1]
    kt = kt.reshape(g_, tc * ch, tc * ch)

    def state_in(d, qr, qi):
        wr = qr[..., None] * bbr[d][None] - qi[..., None] * bbi[d][None]
        wi = qr[..., None] * bbi[d][None] + qi[..., None] * bbr[d][None]
        wr = jnp.transpose(wr, (1, 0, 3, 2)).reshape(g_, tc * ch, p_)
        wi = jnp.transpose(wi, (1, 0, 3, 2)).reshape(g_, tc * ch, p_)
        return jnp.concatenate([wr, wi, wi, wr], axis=-1)
    ws = jnp.concatenate([state_in(0, pr[:tc, 0][::-1], pi[:tc, 0][::-1]),
                          state_in(1, pr[:tc, 1], pi[:tc, 1])], axis=-1)

    def state_out(gr, gi):
        top = jnp.transpose(gr, (1, 0, 2, 3)).reshape(g_, tc * ch, p_)
        bot = jnp.transpose(-gi, (1, 0, 2, 3)).reshape(g_, tc * ch, p_)
        return jnp.concatenate([top, bot], axis=-1)
    wot = jnp.concatenate([state_out(cpr[1:, 0], cpi[1:, 0]),
                           state_out(cpr[1:, 1][::-1], cpi[1:, 1][::-1])], axis=-1)

    acr, aci = pr[tc], pi[tc]
    half = SUBLANES // 2
    rows = []
    for re_part, im_part in ((acr, acr), (-aci, aci), (aci, -aci)):
        both = jnp.concatenate([re_part, im_part], axis=-1)
        rows.append(jnp.concatenate([jnp.repeat(both[0][:, None], half, axis=1),
                                     jnp.repeat(both[1][:, None], half, axis=1)], axis=1))
    coef = jnp.stack(rows, axis=1)
    return kt.astype(BF16), ws.astype(BF16), wot.astype(BF16), coef.astype(F32)


def _s5_mixer(du, mats, ctx_len):
    nb, t_all, w = du.shape
    tc = S5_CHUNK
    half = SUBLANES // 2
    g = mats[0].shape[0]
    nch = t_all // tc
    r = -(-half * nch // 128) * 128
    u = jnp.pad(du.reshape(nb * nch, tc * w), ((0, r - nb * nch), (0, 0)))
    ut = u.T.reshape(tc, g, w // g, r)
    yt = _s5_call(ut, *mats, nch, ctx_len // tc)
    return yt.reshape(tc * w, r).T[:nb * nch].reshape(nb, t_all, w)


def _outproj_kernel(ya_ref, yb_ref, yc_ref, ys_ref, du_ref, gate_ref, x_ref, ctx_ref, mod_ref, dskip_ref,
                    wglu_ref, wout_ref, lng_ref, lnb_ref, o_ref, octx_ref, *, n_batch, d_model, alpha):
    b = pl.program_id(0)
    t = pl.program_id(1)
    row = jnp.where(t == 0, n_batch, b)
    gate_vec = mod_ref[pl.ds(row, 1), 2 * d_model:3 * d_model]

    yd = ys_ref[0] + dskip_ref[...] * du_ref[0].astype(F32)
    zz = _dot(jax.nn.gelu(yd).astype(BF16), wglu_ref[...])
    yd = zz[:, 0:256] * jax.nn.sigmoid(zz[:, 256:512])
    gates = gate_ref[0]
    parts = [ya_ref[0], yb_ref[0], yc_ref[0]]
    cat = [parts[i] * gates[:, 256 * i:256 * (i + 1)] for i in range(3)]
    cat.append((yd * gates[:, 768:1024].astype(F32)).astype(BF16))
    y = _dot(jnp.concatenate(cat, axis=1), wout_ref[...])
    v = alpha * jnp.where(t == 0, ctx_ref[0], x_ref[0]) + gate_vec * y
    mu = jnp.mean(v, axis=1, keepdims=True)
    vc = v - mu
    var = jnp.mean(vc * vc, axis=1, keepdims=True)
    res = vc * lax.rsqrt(var + EPS_LN) * lng_ref[...] + lnb_ref[...]
    o_ref[0] = res

    @pl.when(t == 0)
    def _():
        octx_ref[0] = res


def _outproj_call(ya, yb, yc, ys, du, gates, x, ctx, mod, dskip, wglu, wout, lng, lnb, alpha):
    nb, seq, d = x.shape
    t_all = seq + ctx.shape[1]
    lat = pl.BlockSpec((1, TOK, d), lambda b, t: (b, jnp.maximum(t - 1, 0), 0))
    cblk = pl.BlockSpec((1, TOK, d), lambda b, t: (b, 0, 0))
    tm = lambda w: pl.BlockSpec((1, TOK, w), lambda b, t: (b, t, 0))
    full = lambda a: pl.BlockSpec(a.shape, lambda b, t: (0,) * a.ndim)
    return pl.pallas_call(
        functools.partial(_outproj_kernel, n_batch=nb, d_model=d, alpha=alpha),
        out_shape=[jax.ShapeDtypeStruct(x.shape, F32), jax.ShapeDtypeStruct(ctx.shape, F32)],
        grid=(nb, t_all // TOK),
        in_specs=[tm(256), tm(256), tm(256), tm(256), tm(256), tm(1024), lat, cblk,
                  full(mod), full(dskip), full(wglu), full(wout), full(lng), full(lnb)],
        out_specs=[lat, cblk],
        compiler_params=pltpu.CompilerParams(dimension_semantics=("parallel", "arbitrary"),
                                             vmem_limit_bytes=40 * MIB),
        name="out_proj",
    )(ya, yb, yc, ys, du, gates, x, ctx, mod, dskip, wglu, wout, lng, lnb)


def _rope_tables(seq, ctx_len):
    t = jnp.arange(seq, dtype=jnp.int32)
    row = (t // GRID_WIDTH).astype(F32)
    col = (t % GRID_WIDTH).astype(F32)

    def tab(half):
        inv = ROPE_THETA ** (-jnp.arange(half, dtype=F32) / half)
        ar, ac = row[:, None] * inv, col[:, None] * inv
        cos = jnp.concatenate([jnp.cos(ar), jnp.cos(ar), jnp.cos(ac), jnp.cos(ac)], axis=1)
        sin = jnp.concatenate([-jnp.sin(ar), jnp.sin(ar), -jnp.sin(ac), jnp.sin(ac)], axis=1)
        cos = jnp.concatenate([jnp.ones((ctx_len, 4 * half), F32), cos], axis=0)
        sin = jnp.concatenate([jnp.zeros((ctx_len, 4 * half), F32), sin], axis=0)
        reps = 128 // (4 * half)
        return jnp.tile(cos, (1, reps)), jnp.tile(sin, (1, reps))

    cosa, sina = tab(HEAD // 4)
    cosb, sinb = tab(HEAD // 8)
    return cosa, sina, cosb, sinb


def kernel(x, c, ctx, c_ctx, w_ada, b_ada, w_in, w_out, ln_g, ln_b, qn_g, kn_g, lam_q1, lam_k1, lam_q2,
           lam_k2, subln_g, na_rpb, s5_a_re, s5_a_im, s5_log_dt, s5_b_re, s5_b_im, s5_c_re, s5_c_im, s5_d,
           w_glu):
    nb, seq, d = x.shape
    ctx_len = ctx.shape[1]
    depth = w_ada.shape[0]
    assert ctx_len == TOK and seq % TOK == 0 and nb < SUBLANES
    alpha = (2 * depth) ** 0.25

    cc = jnp.concatenate([c, c_ctx[None], jnp.zeros((SUBLANES - nb - 1, d), F32)], axis=0)
    mods = _ada_call(cc, w_ada, b_ada)
    tabs = _rope_tables(seq, ctx_len)
    n_rows = seq // GRID_WIDTH

    w_in_b, w_out_b, w_glu_b = w_in.astype(BF16), w_out.astype(BF16), w_glu.astype(BF16)
    na_bias = jax.vmap(lambda rpb: _na_bias(rpb, n_rows))(na_rpb)
    s5_mats = jax.vmap(_s5_matrices)(s5_a_re, s5_a_im, s5_log_dt, s5_b_re, s5_b_im, s5_c_re, s5_c_im)
    qg, kg, sg = (jnp.tile(v[:, None], (1, 1, 2)) for v in (qn_g, kn_g, subln_g))

    for l in range(depth):
        lam_init = 0.8 - 0.6 * math.exp(-0.3 * l)
        (qat, ka, vat, qbt, kb, vbt, qct, kc, vct, gates, du) = _inproj_call(
            x, ctx, mods[l], w_in_b[l], tabs, qg[l], kg[l])
        ya = _gqa_call(qat, ka, vat)
        yb = _diff_call(qbt, kb, vbt, lam_q1[l][None], lam_k1[l][None], lam_q2[l][None], lam_k2[l][None],
                        sg[l], lam_init)
        yc = _na_call(qct, kc, vct, na_bias[l])
        ys = _s5_mixer(du, tuple(m[l] for m in s5_mats), ctx_len)
        x, ctx = _outproj_call(ya, yb, yc, ys, du, gates, x, ctx, mods[l], s5_d[l][None], w_glu_b[l],
                               w_out_b[l], ln_g[l][None], ln_b[l][None], alpha)
    return x
```

```python
import functools
import math

import jax
import jax.numpy as jnp
import numpy as np
from jax import lax
from jax.experimental import pallas as pl
from jax.experimental.pallas import tpu as pltpu

F32 = jnp.float32
BF16 = jnp.bfloat16

GRID_WIDTH = 64
HEAD = 64
N_HEADS = 4
A_KV = 2
NA_ROWS = 8
NA_COLS = 16
ROPE_THETA = 10000.0
EPS_RMS = 1e-6
EPS_LN = 1e-5

TOK = 256
KCHUNK = 512
S5_CHUNK = 16
SUBLANES = 8
NEG = -1e30
LOG2E = math.log2(math.e)
MIB = 1024 * 1024


def _cparams(n_axes, vmem_mib):
    return pltpu.CompilerParams(dimension_semantics=("parallel",) * n_axes,
                                vmem_limit_bytes=vmem_mib * MIB)


def _dot(a, b):
    return jnp.dot(a, b, preferred_element_type=F32)


def _ada_kernel(c_ref, w_ref, b_ref, o_ref):
    c = c_ref[...]
    s = c * jax.nn.sigmoid(c)
    o_ref[0] = jnp.dot(s, w_ref[0], preferred_element_type=F32,
                       precision=lax.Precision.HIGHEST) + b_ref[0]


def _ada_call(cc, w_ada, b_ada):
    depth, d, d3 = w_ada.shape
    nblk = 3
    bw = d3 // nblk
    return pl.pallas_call(
        _ada_kernel,
        out_shape=jax.ShapeDtypeStruct((depth, SUBLANES, d3), F32),
        grid=(depth, nblk),
        in_specs=[pl.BlockSpec((SUBLANES, d), lambda l, n: (0, 0)),
                  pl.BlockSpec((1, d, bw), lambda l, n: (l, 0, n)),
                  pl.BlockSpec((1, 1, bw), lambda l, n: (l, 0, n))],
        out_specs=pl.BlockSpec((1, SUBLANES, bw), lambda l, n: (l, 0, n)),
        compiler_params=_cparams(2, 32),
        name="ada_mod",
    )(cc, w_ada, b_ada.reshape(depth, 1, d3))


def _swap_halves(x, half):
    out = []
    lane = lax.broadcasted_iota(jnp.int32, (1, 128), 1)
    first = (lane % (2 * half)) < half
    for s in range(x.shape[1] // 128):
        xs = x[:, 128 * s:128 * (s + 1)]
        fwd = pltpu.roll(xs, 128 - half, axis=1)
        bwd = pltpu.roll(xs, half, axis=1)
        out.append(jnp.where(first, fwd, bwd))
    return out[0] if len(out) == 1 else jnp.concatenate(out, axis=1)


def _tile_lanes(t, reps):
    return t if reps == 1 else jnp.concatenate([t] * reps, axis=1)


def _head_meansq(x):
    w = x.shape[1]
    r = lax.broadcasted_iota(jnp.int32, (w, w), 0) // HEAD
    c = lax.broadcasted_iota(jnp.int32, (w, w), 1) // HEAD
    bd = jnp.where(r == c, 1.0, 0.0).astype(BF16)
    xx = x * x
    hi = xx.astype(BF16)
    lo = (xx - hi.astype(F32)).astype(BF16)
    return (_dot(hi, bd) + _dot(lo, bd)) * (1.0 / HEAD)


def _inproj_kernel(x_ref, ctx_ref, mod_ref, w_ref, cosa_ref, sina_ref, cosb_ref, sinb_ref, qg_ref, kg_ref,
                   qat_ref, ka_ref, vat_ref, qbt_ref, kb_ref, vbt_ref, qct_ref, kc_ref, vct_ref,
                   gate_ref, du_ref, *, n_batch, d_model):
    b = pl.program_id(0)
    t = pl.program_id(1)
    row = jnp.where(t == 0, n_batch, b)
    mod = mod_ref[pl.ds(row, 1), :]
    shift = mod[:, :d_model]
    scale = mod[:, d_model:2 * d_model]
    xin = jnp.where(t == 0, ctx_ref[0], x_ref[0])
    h = (xin * (1.0 + scale) + shift).astype(BF16)
    z = _dot(h, w_ref[...])

    cosa, sina = cosa_ref[...], sina_ref[...]
    cosb, sinb = cosb_ref[...], sinb_ref[...]

    aq = z[:, 0:256]
    aq = aq * lax.rsqrt(_head_meansq(aq) + EPS_RMS) * _tile_lanes(qg_ref[...], 2)
    aq = aq * _tile_lanes(cosa, 2) + _swap_halves(aq, 16) * _tile_lanes(sina, 2)
    qat_ref[0] = (aq * (HEAD ** -0.5 * LOG2E)).T.astype(BF16)
    ak = z[:, 256:384]
    ak = ak * lax.rsqrt(_head_meansq(ak) + EPS_RMS) * kg_ref[...]
    ak = ak * cosa + _swap_halves(ak, 16) * sina
    ka_ref[0] = ak.astype(BF16)
    vat_ref[0] = z[:, 384:512].T.astype(BF16)

    bq = z[:, 768:1024]
    bq = bq * _tile_lanes(cosb, 2) + _swap_halves(bq, 8) * _tile_lanes(sinb, 2)
    qbt_ref[0] = (bq * ((HEAD // 2) ** -0.5 * LOG2E)).T.astype(BF16)
    bk = z[:, 1024:1280]
    bk = bk * _tile_lanes(cosb, 2) + _swap_halves(bk, 8) * _tile_lanes(sinb, 2)
    kb_ref[0] = bk.astype(BF16)
    vbt_ref[0] = z[:, 1280:1536].T.astype(BF16)

    qct_ref[0] = (z[:, 1792:2048] * (HEAD ** -0.5 * LOG2E)).T.astype(BF16)
    kc_ref[0] = z[:, 2048:2304].astype(BF16)
    vct_ref[0] = z[:, 2304:2560].T.astype(BF16)

    du_ref[0] = z[:, 2816:3072].astype(BF16)
    for i, c0 in enumerate((512, 1536, 2560, 3072)):
        g = z[:, c0:c0 + 256]
        gate_ref[0, :, 256 * i:256 * (i + 1)] = (g * jax.nn.sigmoid(g)).astype(BF16)


def _inproj_call(x, ctx, mod, w_in, tabs, qg, kg):
    nb, seq, d = x.shape
    t_all = seq + ctx.shape[1]
    nt = t_all // TOK
    win = w_in.shape[1]
    tok_major = lambda w: (jax.ShapeDtypeStruct((nb, t_all, w), BF16),
                           pl.BlockSpec((1, TOK, w), lambda b, t: (b, t, 0)))
    feat_major = lambda w: (jax.ShapeDtypeStruct((nb, w, t_all), BF16),
                            pl.BlockSpec((1, w, TOK), lambda b, t: (b, 0, t)))
    outs = [feat_major(256), tok_major(128), feat_major(128),
            feat_major(256), tok_major(256), feat_major(256),
            feat_major(256), tok_major(256), feat_major(256),
            tok_major(1024), tok_major(256)]
    tab_spec = pl.BlockSpec((TOK, 128), lambda b, t: (t, 0))
    return pl.pallas_call(
        functools.partial(_inproj_kernel, n_batch=nb, d_model=d),
        out_shape=[o[0] for o in outs],
        grid=(nb, nt),
        in_specs=[pl.BlockSpec((1, TOK, d), lambda b, t: (b, jnp.maximum(t - 1, 0), 0)),
                  pl.BlockSpec((1, TOK, d), lambda b, t: (b, 0, 0)),
                  pl.BlockSpec((SUBLANES, 3 * d), lambda b, t: (0, 0)),
                  pl.BlockSpec((d, win), lambda b, t: (0, 0)),
                  tab_spec, tab_spec, tab_spec, tab_spec,
                  pl.BlockSpec((1, 128), lambda b, t: (0, 0)),
                  pl.BlockSpec((1, 128), lambda b, t: (0, 0))],
        out_specs=[o[1] for o in outs],
        compiler_params=_cparams(2, 56),
        name="in_proj",
    )(x, ctx, mod, w_in, *tabs, qg, kg)


def _pad_rows(x, start, total):
    parts = []
    if start:
        parts.append(jnp.zeros((start, x.shape[1]), x.dtype))
    parts.append(x)
    rest = total - start - x.shape[0]
    if rest:
        parts.append(jnp.zeros((rest, x.shape[1]), x.dtype))
    return jnp.concatenate(parts, axis=0) if len(parts) > 1 else x


def _colmax(parts):
    parts = list(parts)
    while len(parts) > 1:
        parts = [jnp.maximum(parts[2 * i], parts[2 * i + 1]) for i in range(len(parts) // 2)] + parts[len(parts) // 2 * 2:]
    y = parts[0]
    while y.shape[0] > SUBLANES:
        half = y.shape[0] // 2
        y = jnp.maximum(y[:half], y[half:])
    return jnp.max(y, axis=0, keepdims=True)


def _flash_maps(k_ref, vt_ref, q_pads, v_rows, latent, t_all, slots):
    bq = q_pads[0].shape[1]
    n = len(q_pads)

    def scores(off, size, slot=None):
        nsub = size // TOK
        ks = [k_ref[0, pl.ds(off + TOK * r, TOK), :] for r in range(nsub)]
        s = [[_dot(ks[r], q_pads[i]) for r in range(nsub)] for i in range(n)]
        cmax = tuple(_colmax(s[i]) for i in range(n))
        if slot is None:
            return s, cmax
        for i in range(n):
            for r in range(nsub):
                slot[i, TOK * r:TOK * (r + 1), :] = s[i][r]
        return cmax

    def accumulate(stats, s, cmax, off, size):
        nsub = size // TOK
        ones = jnp.ones((16, size), BF16)
        v1 = {}
        new = []
        for i in range(n):
            key = (v_rows[i].start, v_rows[i].stop)
            if key not in v1:
                v1[key] = jnp.concatenate([vt_ref[0, v_rows[i], pl.ds(off, size)], ones], axis=0)
            m_new = jnp.maximum(stats[2 * i], cmax[i])
            p = []
            for r in range(nsub):
                blk = s[i][r] if isinstance(s, list) else s[i, TOK * r:TOK * (r + 1), :]
                p.append(jnp.exp2(blk - m_new).astype(BF16))
            p = p[0] if nsub == 1 else jnp.concatenate(p, axis=0)
            acc = stats[2 * i + 1] * jnp.exp2(stats[2 * i] - m_new) + _dot(v1[key], p)
            new += [m_new, acc]
        return tuple(new)

    stats = []
    for _ in range(n):
        stats += [jnp.full((1, bq), NEG, F32), jnp.zeros((HEAD + 16, bq), F32)]
    stats = tuple(stats)
    s_ctx, c_ctx = scores(0, TOK)
    if not latent:
        stats = accumulate(stats, s_ctx, c_ctx, 0, TOK)
    else:
        s0, s1 = slots
        n_lat = (t_all - TOK) // KCHUNK
        assert n_lat % 2 == 0
        lat_off = lambda c: pl.multiple_of(TOK + c * KCHUNK, TOK)
        c0 = scores(TOK, KCHUNK, s0)
        stats = accumulate(stats, s_ctx, c_ctx, 0, TOK)

        def body(k, carry):
            stats, c0 = carry
            c1 = scores(lat_off(2 * k + 1), KCHUNK, s1)
            stats = accumulate(stats, s0, c0, lat_off(2 * k), KCHUNK)
            c0 = scores(lat_off(2 * k + 2), KCHUNK, s0)
            stats = accumulate(stats, s1, c1, lat_off(2 * k + 1), KCHUNK)
            return stats, c0

        stats, c0 = lax.fori_loop(0, n_lat // 2 - 1, body, (stats, c0))
        c1 = scores(lat_off(n_lat - 1), KCHUNK, s1)
        stats = accumulate(stats, s0, c0, lat_off(n_lat - 2), KCHUNK)
        stats = accumulate(stats, s1, c1, lat_off(n_lat - 1), KCHUNK)
    return [stats[2 * i + 1][0:HEAD] * (1.0 / stats[2 * i + 1][HEAD:HEAD + 1]) for i in range(n)]


def _gqa_kernel(qt_ref, k_ref, vt_ref, o_ref, s0_ref, s1_ref, *, t_all):
    def run(latent):
        rep = N_HEADS // A_KV
        q_pads = [_pad_rows(qt_ref[0, HEAD * h:HEAD * (h + 1), :], HEAD * (h // rep), HEAD * A_KV)
                  for h in range(N_HEADS)]
        v_rows = [slice(HEAD * (h // rep), HEAD * (h // rep + 1)) for h in range(N_HEADS)]
        outs = _flash_maps(k_ref, vt_ref, q_pads, v_rows, latent, t_all, (s0_ref, s1_ref))
        o_ref[0] = jnp.concatenate(outs, axis=0).T.astype(BF16)

    j = pl.program_id(1)
    pl.when(j == 0)(lambda: run(False))
    pl.when(j > 0)(lambda: run(True))


def _gqa_call(qat, ka, vat):
    nb, _, t_all = qat.shape
    return pl.pallas_call(
        functools.partial(_gqa_kernel, t_all=t_all),
        out_shape=jax.ShapeDtypeStruct((nb, t_all, 256), BF16),
        grid=(nb, t_all // TOK),
        in_specs=[pl.BlockSpec((1, 256, TOK), lambda b, j: (b, 0, j)),
                  pl.BlockSpec((1, t_all, 128), lambda b, j: (b, 0, 0)),
                  pl.BlockSpec((1, 128, t_all), lambda b, j: (b, 0, 0))],
        out_specs=pl.BlockSpec((1, TOK, 256), lambda b, j: (b, j, 0)),
        scratch_shapes=[pltpu.VMEM((N_HEADS, KCHUNK, TOK), F32)] * 2,
        compiler_params=_cparams(2, 32),
        name="gqa_attn",
    )(qat, ka, vat)


def _diff_kernel(qt_ref, k_ref, vt_ref, lq1_ref, lk1_ref, lq2_ref, lk2_ref, g_ref, o_ref, s0_ref, s1_ref,
                 *, t_all, lam_init):
    lam = (jnp.exp(jnp.sum(lq1_ref[...] * lk1_ref[...], axis=1, keepdims=True))
           - jnp.exp(jnp.sum(lq2_ref[...] * lk2_ref[...], axis=1, keepdims=True)) + lam_init)
    half = HEAD // 2

    def run(latent):
        q_pads, v_rows = [], []
        for h in range(N_HEADS):
            q_pads.append(_pad_rows(qt_ref[0, HEAD * h:HEAD * h + half, :], HEAD * h, HEAD * N_HEADS))
            q_pads.append(_pad_rows(qt_ref[0, HEAD * h + half:HEAD * (h + 1), :], HEAD * h + half,
                                    HEAD * N_HEADS))
            v_rows += [slice(HEAD * h, HEAD * (h + 1))] * 2
        res = _flash_maps(k_ref, vt_ref, q_pads, v_rows, latent, t_all, (s0_ref, s1_ref))
        outs = []
        for h in range(N_HEADS):
            o = res[2 * h] - lam * res[2 * h + 1]
            ms = jnp.mean(o * o, axis=0, keepdims=True)
            outs.append(o * lax.rsqrt(ms + EPS_RMS))
        y = jnp.concatenate(outs, axis=0).T * _tile_lanes(g_ref[...], 2) * (1.0 - lam_init)
        o_ref[0] = y.astype(BF16)

    j = pl.program_id(1)
    pl.when(j == 0)(lambda: run(False))
    pl.when(j > 0)(lambda: run(True))


def _diff_call(qbt, kb, vbt, lq1, lk1, lq2, lk2, g128, lam_init):
    nb, _, t_all = qbt.shape
    vec = pl.BlockSpec((1, HEAD // 2), lambda b, j: (0, 0))
    return pl.pallas_call(
        functools.partial(_diff_kernel, t_all=t_all, lam_init=lam_init),
        out_shape=jax.ShapeDtypeStruct((nb, t_all, 256), BF16),
        grid=(nb, t_all // TOK),
        in_specs=[pl.BlockSpec((1, 256, TOK), lambda b, j: (b, 0, j)),
                  pl.BlockSpec((1, t_all, 256), lambda b, j: (b, 0, 0)),
                  pl.BlockSpec((1, 256, t_all), lambda b, j: (b, 0, 0)),
                  vec, vec, vec, vec,
                  pl.BlockSpec((1, 128), lambda b, j: (0, 0))],
        out_specs=pl.BlockSpec((1, TOK, 256), lambda b, j: (b, j, 0)),
        scratch_shapes=[pltpu.VMEM((2 * N_HEADS, KCHUNK, TOK), F32)] * 2,
        compiler_params=_cparams(2, 40),
        name="diff_attn",
    )(qbt, kb, vbt, lq1, lk1, lq2, lk2, g128)


NA_WIN = 3


def _na_kernel(qt_ref, k0_ref, k1_ref, k2_ref, kc_ref, v0_ref, v1_ref, v2_ref, vc_ref, bias_ref, o_ref):
    kwin = jnp.concatenate([k0_ref[0], k1_ref[0], k2_ref[0]], axis=0)
    kctx = kc_ref[0]
    ones = jnp.ones((16, TOK), BF16)
    heads = range(N_HEADS)
    rows = [slice(HEAD * h, HEAD * (h + 1)) for h in heads]
    qpad = [_pad_rows(qt_ref[0, rows[h], :], HEAD * h, HEAD * N_HEADS) for h in heads]
    s_win = [_dot(kwin, qpad[h]) + bias_ref[h, 0] for h in heads]
    s_ctx = [_dot(kctx, qpad[h]) for h in heads]
    m = [_colmax([s_win[h][TOK * r:TOK * (r + 1)] for r in range(NA_WIN)] + [s_ctx[h]]) for h in heads]
    outs = []
    for h in heads:
        p_win = jnp.exp2(s_win[h] - m[h]).astype(BF16)
        p_ctx = jnp.exp2(s_ctx[h] - m[h]).astype(BF16)
        vwin = jnp.concatenate([v0_ref[0, rows[h], :], v1_ref[0, rows[h], :], v2_ref[0, rows[h], :]], axis=1)
        acc = (_dot(jnp.concatenate([vwin, jnp.concatenate([ones] * NA_WIN, axis=1)], axis=0), p_win)
               + _dot(jnp.concatenate([vc_ref[0, rows[h], :], ones], axis=0), p_ctx))
        outs.append(acc[0:HEAD] * (1.0 / acc[HEAD:HEAD + 1]))
    o_ref[0] = jnp.concatenate(outs, axis=0).T.astype(BF16)


def _na_call(qct, kc, vct, bias):
    nb, _, t_all = qct.shape
    nt = t_all // TOK
    n_lat = nt - 1

    def win(i):
        return lambda b, j: 1 + jnp.clip(j - 2, 0, n_lat - NA_WIN) + i

    def pattern(b, j):
        return jnp.where(j == 0, 3, jnp.where(j == 1, 0, jnp.where(j == nt - 1, 2, 1)))

    k_specs = [pl.BlockSpec((1, TOK, 256), (lambda b, j, f=win(i): (b, f(b, j), 0))) for i in range(NA_WIN)]
    v_specs = [pl.BlockSpec((1, 256, TOK), (lambda b, j, f=win(i): (b, 0, f(b, j)))) for i in range(NA_WIN)]
    return pl.pallas_call(
        _na_kernel,
        out_shape=jax.ShapeDtypeStruct((nb, t_all, 256), BF16),
        grid=(nb, nt),
        in_specs=[pl.BlockSpec((1, 256, TOK), lambda b, j: (b, 0, j))]
                 + k_specs + [pl.BlockSpec((1, TOK, 256), lambda b, j: (b, 0, 0))]
                 + v_specs + [pl.BlockSpec((1, 256, TOK), lambda b, j: (b, 0, 0))]
                 + [pl.BlockSpec((N_HEADS, 1, NA_WIN * TOK, TOK), lambda b, j: (0, pattern(b, j), 0, 0))],
        out_specs=pl.BlockSpec((1, TOK, 256), lambda b, j: (b, j, 0)),
        compiler_params=_cparams(2, 48),
        name="na_attn",
    )(qct, kc, kc, kc, kc, vct, vct, vct, vct, bias)


def _na_bias_kernel(t1_ref, o_ref, *, plans):
    neg = jnp.full((GRID_WIDTH, GRID_WIDTH), NEG, F32)
    for p, plan in enumerate(plans):
        for h in range(N_HEADS):
            for kr, drs in enumerate(plan):
                slab = jnp.concatenate([neg if dr is None else t1_ref[0, h, dr] for dr in drs], axis=1)
                o_ref[0, h, p, GRID_WIDTH * kr:GRID_WIDTH * (kr + 1), :] = slab
    o_ref[0, :, len(plans)] = jnp.full((N_HEADS,) + o_ref.shape[3:], NEG, F32)


def _na_bias_all(rpb, n_rows):
    nl, nh = rpb.shape[:2]
    tile_rows = TOK // GRID_WIDTH
    win_rows = NA_WIN * tile_rows
    n_lat = n_rows // tile_rows
    kc = np.arange(GRID_WIDTH)[:, None]
    qc = np.arange(GRID_WIDTH)[None, :]
    cs = np.clip(qc - NA_COLS // 2, 0, GRID_WIDTH - NA_COLS)
    col_ok = (kc >= cs) & (kc < cs + NA_COLS)
    dc = kc - qc + NA_COLS - 1
    e_col = ((dc[None] == np.arange(2 * NA_COLS - 1)[:, None, None]) & col_ok[None]).astype(np.float32)
    t1 = jnp.dot(rpb.reshape(-1, 2 * NA_COLS - 1), e_col.reshape(2 * NA_COLS - 1, -1),
                 precision=lax.Precision.HIGHEST)
    t1 = jnp.where(col_ok.reshape(-1)[None], t1 * LOG2E, NEG)
    t1 = t1.reshape(nl, nh, 2 * NA_ROWS - 1, GRID_WIDTH, GRID_WIDTH)

    def plan(r0, w0):
        out = []
        for kr in range(win_rows):
            drs = []
            for qr in range(tile_rows):
                rs = min(max(r0 + qr - NA_ROWS // 2, 0), n_rows - NA_ROWS)
                ok = rs <= w0 + kr < rs + NA_ROWS
                drs.append((w0 + kr) - (r0 + qr) + NA_ROWS - 1 if ok else None)
            out.append(drs)
        return out

    last = n_lat - 1
    plans = [plan(0, 0), plan(tile_rows, 0), plan(last * tile_rows, (last - NA_WIN + 1) * tile_rows)]
    shape = (nl, nh, len(plans) + 1, win_rows * GRID_WIDTH, TOK)
    return pl.pallas_call(
        functools.partial(_na_bias_kernel, plans=plans),
        out_shape=jax.ShapeDtypeStruct(shape, F32),
        grid=(nl,),
        in_specs=[pl.BlockSpec((1,) + t1.shape[1:], lambda l: (l, 0, 0, 0, 0))],
        out_specs=pl.BlockSpec((1,) + shape[1:], lambda l: (l, 0, 0, 0, 0)),
        compiler_params=_cparams(1, 48),
        name="na_bias",
    )(t1)


def _s5_kernel(u_ref, kt_ref, ws_ref, wot_ref, coef_ref, y_ref, s_scr, ssw_scr, hf_scr, hb_scr, *,
               n_chunks, ctx_chunks):
    tc = S5_CHUNK
    half = SUBLANES // 2
    ut = jnp.concatenate([u_ref[i, 0] for i in range(tc)], axis=0)
    u = ut.astype(F32).T.astype(BF16)
    s_all = _dot(u, ws_ref[0])
    for b in range(half):
        rows = slice(b * n_chunks, (b + 1) * n_chunks)
        s_scr[pl.ds(b, n_chunks, stride=SUBLANES), :] = s_all[rows, 0:128]
        ssw_scr[pl.ds(b, n_chunks, stride=SUBLANES), :] = s_all[rows, 128:256]
        s_scr[pl.ds(half + b, n_chunks, stride=SUBLANES), :] = s_all[rows, 256:384]
        ssw_scr[pl.ds(half + b, n_chunks, stride=SUBLANES), :] = s_all[rows, 384:512]
    c1, c2, c3 = coef_ref[0, 0], coef_ref[0, 1], coef_ref[0, 2]
    is_fwd = lax.broadcasted_iota(jnp.int32, (SUBLANES, 128), 0) < half

    def body(k, carry):
        h, hs = carry
        kb = jnp.where(k < ctx_chunks, ctx_chunks - 1 - k, n_chunks - 1 + ctx_chunks - k)
        rf = pl.multiple_of(k * SUBLANES, SUBLANES)
        rb = pl.multiple_of(kb * SUBLANES, SUBLANES)
        hf_scr[pl.ds(rf, SUBLANES), :] = h
        hb_scr[pl.ds(rb, SUBLANES), :] = h
        sk = jnp.where(is_fwd, s_scr[pl.ds(rf, SUBLANES), :], s_scr[pl.ds(rb, SUBLANES), :])
        ssk = jnp.where(is_fwd, ssw_scr[pl.ds(rf, SUBLANES), :], ssw_scr[pl.ds(rb, SUBLANES), :])
        return h * c1 + hs * c2 + sk, hs * c1 + h * c3 + ssk

    zero = jnp.zeros((SUBLANES, 128), F32)
    lax.fori_loop(0, n_chunks, body, (zero, zero), unroll=4)
    pad = jnp.zeros((u.shape[0] - half * n_chunks, 256), F32)
    hcat = jnp.concatenate(
        [jnp.concatenate([hf_scr[pl.ds(b, n_chunks, stride=SUBLANES), :],
                          hb_scr[pl.ds(half + b, n_chunks, stride=SUBLANES), :]], axis=1)
         for b in range(half)] + ([pad] if pad.shape[0] else []), axis=0)
    nt = (((1,), (1,)), ((), ()))
    y = _dot(kt_ref[0], ut) + lax.dot_general(wot_ref[0], hcat.astype(BF16), nt, preferred_element_type=F32)
    for t in range(tc):
        y_ref[t, 0] = y[tc * t:tc * (t + 1)]


def _s5_call(ut, kt, ws, wot, coef, n_chunks, ctx_chunks):
    tc, g, ch, r = ut.shape
    w = tc * ch
    grp = lambda *shape: pl.BlockSpec((1,) + shape, lambda i: (i,) + (0,) * len(shape))
    io = pl.BlockSpec((tc, 1, ch, r), lambda i: (0, i, 0, 0))
    tile_rows = n_chunks * SUBLANES
    return pl.pallas_call(
        functools.partial(_s5_kernel, n_chunks=n_chunks, ctx_chunks=ctx_chunks),
        out_shape=jax.ShapeDtypeStruct((tc, g, ch, r), F32),
        grid=(g,),
        in_specs=[io, grp(w, w), grp(w, 512), grp(w, w), grp(3, SUBLANES, 128)],
        out_specs=io,
        scratch_shapes=[pltpu.VMEM((tile_rows, 128), F32)] * 4,
        compiler_params=_cparams(1, 48),
        name="s5_scan",
    )(ut, kt, ws, wot, coef)


def _s5_param_kernel(are_ref, aim_ref, ldt_ref, btr_ref, bti_ref, cr_ref, ci_ref,
                     kt_ref, ws_ref, wot_ref, coef_ref):
    tc = S5_CHUNK
    hp = lax.Precision.HIGHEST
    nt = (((1,), (1,)), ((), ()))
    kt_parts, ws_parts, wot_parts, coef_parts = [], [], [], []
    for d in range(2):
        a_re, a_im = are_ref[0, 0, d:d + 1, :], aim_ref[0, 0, d:d + 1, :]
        dt = jnp.exp(ldt_ref[0, 0, d:d + 1, :])
        j = lax.broadcasted_iota(jnp.int32, (2 * tc, 1), 0).astype(F32)
        mag = jnp.exp(j * (a_re * dt))
        ang = j * (a_im * dt)
        pr, pi = mag * jnp.cos(ang), mag * jnp.sin(ang)
        nr, ni = pr[1:2] - 1.0, pi[1:2]
        den = a_re * a_re + a_im * a_im
        fr, fi = (nr * a_re + ni * a_im) / den, (ni * a_re - nr * a_im) / den
        btr, bti = btr_ref[0, 0, d], bti_ref[0, 0, d]
        bbr, bbi = fr * btr - fi * bti, fr * bti + fi * btr
        c_r, c_i = cr_ref[0, 0, d], ci_ref[0, 0, d]
        wr, wi, gr, gi = [], [], [], []
        for pos in range(tc):
            k = tc - 1 - pos if d == 0 else pos
            wr.append(bbr * pr[k:k + 1] - bbi * pi[k:k + 1])
            wi.append(bbr * pi[k:k + 1] + bbi * pr[k:k + 1])
            k = pos + 1 if d == 0 else tc - pos
            gr.append(c_r * pr[k:k + 1] - c_i * pi[k:k + 1])
            gi.append(c_r * pi[k:k + 1] + c_i * pr[k:k + 1])
        wr, wi, gr, gi = (jnp.concatenate(v, axis=0) for v in (wr, wi, gr, gi))
        ws_parts += [wr, wi, wi, wr]
        wot_parts += [gr, -gi]
        row = lax.dot_general(jnp.concatenate([c_r, -c_i], axis=1), jnp.concatenate([wr, wi], axis=1), nt,
                              precision=hp, preferred_element_type=F32)
        ch = row.shape[0]
        blocks = []
        for t in range(tc):
            sh = ch * (tc - 1 - t) if d == 0 else ch * t
            if sh == 0:
                blocks.append(row)
            elif d == 0:
                blocks.append(jnp.concatenate([row[:, sh:], jnp.zeros((ch, sh), F32)], axis=1))
            else:
                blocks.append(jnp.concatenate([jnp.zeros((ch, sh), F32), row[:, :tc * ch - sh]], axis=1))
        kt_parts.append(jnp.concatenate(blocks, axis=0))
        a16r, a16i = pr[tc:tc + 1], pi[tc:tc + 1]
        coef_parts.append([jnp.concatenate([a16r, a16r], axis=1), jnp.concatenate([-a16i, a16i], axis=1),
                           jnp.concatenate([a16i, -a16i], axis=1)])
    kt_ref[0, 0] = (kt_parts[0] + kt_parts[1]).astype(BF16)
    ws_ref[0, 0] = jnp.concatenate(ws_parts, axis=1).astype(BF16)
    wot_ref[0, 0] = jnp.concatenate(wot_parts, axis=1).astype(BF16)
    half = SUBLANES // 2
    for i in range(3):
        coef_ref[0, 0, i] = jnp.concatenate([jnp.broadcast_to(coef_parts[0][i], (half, 128)),
                                             jnp.broadcast_to(coef_parts[1][i], (half, 128))], axis=0)


def _s5_matrices(a_re, a_im, log_dt, b_re, b_im, c_re, c_im):
    nl, _, g, p = a_re.shape
    ch = b_re.shape[-1]
    w = S5_CHUNK * ch
    lgd = lambda a: jnp.moveaxis(a, 1, 2)
    ins = [lgd(a_re), lgd(a_im), lgd(log_dt)[..., None], lgd(jnp.swapaxes(b_re, -1, -2)),
           lgd(jnp.swapaxes(b_im, -1, -2)), lgd(c_re), lgd(c_im)]
    blk = lambda *shape: pl.BlockSpec((1, 1) + shape, lambda l, i: (l, i) + (0,) * len(shape))
    return pl.pallas_call(
        _s5_param_kernel,
        out_shape=[jax.ShapeDtypeStruct((nl, g, w, w), BF16), jax.ShapeDtypeStruct((nl, g, w, 512), BF16),
                   jax.ShapeDtypeStruct((nl, g, w, w), BF16),
                   jax.ShapeDtypeStruct((nl, g, 3, SUBLANES, 128), F32)],
        grid=(nl, g),
        in_specs=[blk(2, p), blk(2, p), blk(2, 1), blk(2, ch, p), blk(2, ch, p), blk(2, ch, p), blk(2, ch, p)],
        out_specs=[blk(w, w), blk(w, 512), blk(w, w), blk(3, SUBLANES, 128)],
        compiler_params=_cparams(2, 32),
        name="s5_params",
    )(*ins)


def _s5_mixer(du, mats, ctx_len):
    nb, t_all, w = du.shape
    tc = S5_CHUNK
    half = SUBLANES // 2
    g = mats[0].shape[0]
    nch = t_all // tc
    r = -(-half * nch // 128) * 128
    u = jnp.pad(du.reshape(nb * nch, tc * w), ((0, r - nb * nch), (0, 0)))
    ut = u.T.reshape(tc, g, w // g, r)
    yt = _s5_call(ut, *mats, nch, ctx_len // tc)
    return yt.reshape(tc * w, r).T[:nb * nch].reshape(nb, t_all, w)


def _outproj_kernel(ya_ref, yb_ref, yc_ref, ys_ref, du_ref, gate_ref, x_ref, ctx_ref, mod_ref, dskip_ref,
                    wglu_ref, wout_ref, lng_ref, lnb_ref, o_ref, octx_ref, *, n_batch, d_model, alpha):
    b = pl.program_id(0)
    t = pl.program_id(1)
    row = jnp.where(t == 0, n_batch, b)
    gate_vec = mod_ref[pl.ds(row, 1), 2 * d_model:3 * d_model]

    yd = ys_ref[0] + dskip_ref[...] * du_ref[0].astype(F32)
    zz = _dot(jax.nn.gelu(yd).astype(BF16), wglu_ref[...])
    yd = zz[:, 0:256] * jax.nn.sigmoid(zz[:, 256:512])
    gates = gate_ref[0]
    parts = [ya_ref[0], yb_ref[0], yc_ref[0]]
    cat = [parts[i] * gates[:, 256 * i:256 * (i + 1)] for i in range(3)]
    cat.append((yd * gates[:, 768:1024].astype(F32)).astype(BF16))
    y = _dot(jnp.concatenate(cat, axis=1), wout_ref[...])
    v = alpha * jnp.where(t == 0, ctx_ref[0], x_ref[0]) + gate_vec * y
    mu = jnp.mean(v, axis=1, keepdims=True)
    vc = v - mu
    var = jnp.mean(vc * vc, axis=1, keepdims=True)
    res = vc * lax.rsqrt(var + EPS_LN) * lng_ref[...] + lnb_ref[...]
    o_ref[0] = res

    @pl.when(t == 0)
    def _():
        octx_ref[0] = res


def _outproj_call(ya, yb, yc, ys, du, gates, x, ctx, mod, dskip, wglu, wout, lng, lnb, alpha):
    nb, seq, d = x.shape
    t_all = seq + ctx.shape[1]
    lat = pl.BlockSpec((1, TOK, d), lambda b, t: (b, jnp.maximum(t - 1, 0), 0))
    cblk = pl.BlockSpec((1, TOK, d), lambda b, t: (b, 0, 0))
    tm = lambda w: pl.BlockSpec((1, TOK, w), lambda b, t: (b, t, 0))
    full = lambda a: pl.BlockSpec(a.shape, lambda b, t: (0,) * a.ndim)
    return pl.pallas_call(
        functools.partial(_outproj_kernel, n_batch=nb, d_model=d, alpha=alpha),
        out_shape=[jax.ShapeDtypeStruct(x.shape, F32), jax.ShapeDtypeStruct(ctx.shape, F32)],
        grid=(nb, t_all // TOK),
        in_specs=[tm(256), tm(256), tm(256), tm(256), tm(256), tm(1024), lat, cblk,
                  full(mod), full(dskip), full(wglu), full(wout), full(lng), full(lnb)],
        out_specs=[lat, cblk],
        compiler_params=pltpu.CompilerParams(dimension_semantics=("parallel", "arbitrary"),
                                             vmem_limit_bytes=40 * MIB),
        name="out_proj",
    )(ya, yb, yc, ys, du, gates, x, ctx, mod, dskip, wglu, wout, lng, lnb)


def _rope_tables(seq, ctx_len):
    t = jnp.arange(seq, dtype=jnp.int32)
    row = (t // GRID_WIDTH).astype(F32)
    col = (t % GRID_WIDTH).astype(F32)

    def tab(half):
        inv = ROPE_THETA ** (-jnp.arange(half, dtype=F32) / half)
        ar, ac = row[:, None] * inv, col[:, None] * inv
        cos = jnp.concatenate([jnp.cos(ar), jnp.cos(ar), jnp.cos(ac), jnp.cos(ac)], axis=1)
        sin = jnp.concatenate([-jnp.sin(ar), jnp.sin(ar), -jnp.sin(ac), jnp.sin(ac)], axis=1)
        cos = jnp.concatenate([jnp.ones((ctx_len, 4 * half), F32), cos], axis=0)
        sin = jnp.concatenate([jnp.zeros((ctx_len, 4 * half), F32), sin], axis=0)
        reps = 128 // (4 * half)
        return jnp.tile(cos, (1, reps)), jnp.tile(sin, (1, reps))

    cosa, sina = tab(HEAD // 4)
    cosb, sinb = tab(HEAD // 8)
    return cosa, sina, cosb, sinb


def kernel(x, c, ctx, c_ctx, w_ada, b_ada, w_in, w_out, ln_g, ln_b, qn_g, kn_g, lam_q1, lam_k1, lam_q2,
           lam_k2, subln_g, na_rpb, s5_a_re, s5_a_im, s5_log_dt, s5_b_re, s5_b_im, s5_c_re, s5_c_im, s5_d,
           w_glu):
    nb, seq, d = x.shape
    ctx_len = ctx.shape[1]
    depth = w_ada.shape[0]
    assert ctx_len == TOK and seq % TOK == 0 and nb < SUBLANES
    alpha = (2 * depth) ** 0.25

    cc = jnp.concatenate([c, c_ctx[None], jnp.zeros((SUBLANES - nb - 1, d), F32)], axis=0)
    mods = _ada_call(cc, w_ada, b_ada)
    tabs = _rope_tables(seq, ctx_len)
    n_rows = seq // GRID_WIDTH

    w_in_b, w_out_b, w_glu_b = w_in.astype(BF16), w_out.astype(BF16), w_glu.astype(BF16)
    na_bias = _na_bias_all(na_rpb, n_rows)
    s5_mats = _s5_matrices(s5_a_re, s5_a_im, s5_log_dt, s5_b_re, s5_b_im, s5_c_re, s5_c_im)
    qg, kg, sg = (jnp.tile(v[:, None], (1, 1, 2)) for v in (qn_g, kn_g, subln_g))

    for l in range(depth):
        lam_init = 0.8 - 0.6 * math.exp(-0.3 * l)
        (qat, ka, vat, qbt, kb, vbt, qct, kc, vct, gates, du) = _inproj_call(
            x, ctx, mods[l], w_in_b[l], tabs, qg[l], kg[l])
        ya = _gqa_call(qat, ka, vat)
        yb = _diff_call(qbt, kb, vbt, lam_q1[l][None], lam_k1[l][None], lam_q2[l][None], lam_k2[l][None],
                        sg[l], lam_init)
        yc = _na_call(qct, kc, vct, na_bias[l])
        ys = _s5_mixer(du, tuple(m[l] for m in s5_mats), ctx_len)
        x, ctx = _outproj_call(ya, yb, yc, ys, du, gates, x, ctx, mods[l], s5_d[l][None], w_glu_b[l],
                               w_out_b[l], ln_g[l][None], ln_b[l][None], alpha)
    return x
```

```python
import functools
import math

import jax
import jax.numpy as jnp
import numpy as np
from jax import lax
from jax.experimental import pallas as pl
from jax.experimental.pallas import tpu as pltpu

F32 = jnp.float32
BF16 = jnp.bfloat16

GRID_WIDTH = 64
HEAD = 64
N_HEADS = 4
A_KV = 2
NA_ROWS = 8
NA_COLS = 16
ROPE_THETA = 10000.0
EPS_RMS = 1e-6
EPS_LN = 1e-5

TOK = 256
KCHUNK = 512
S5_CHUNK = 16
SUBLANES = 8
NEG = -1e30
LOG2E = math.log2(math.e)
MIB = 1024 * 1024


def _cparams(n_axes, vmem_mib):
    return pltpu.CompilerParams(dimension_semantics=("parallel",) * n_axes,
                                vmem_limit_bytes=vmem_mib * MIB)


def _dot(a, b):
    return jnp.dot(a, b, preferred_element_type=F32)


def _ada_kernel(c_ref, w_ref, b_ref, o_ref):
    c = c_ref[...]
    s = c * jax.nn.sigmoid(c)
    o_ref[0] = jnp.dot(s, w_ref[0], preferred_element_type=F32,
                       precision=lax.Precision.HIGHEST) + b_ref[0]


def _ada_call(cc, w_ada, b_ada):
    depth, d, d3 = w_ada.shape
    nblk = 3
    bw = d3 // nblk
    return pl.pallas_call(
        _ada_kernel,
        out_shape=jax.ShapeDtypeStruct((depth, SUBLANES, d3), F32),
        grid=(depth, nblk),
        in_specs=[pl.BlockSpec((SUBLANES, d), lambda l, n: (0, 0)),
                  pl.BlockSpec((1, d, bw), lambda l, n: (l, 0, n)),
                  pl.BlockSpec((1, 1, bw), lambda l, n: (l, 0, n))],
        out_specs=pl.BlockSpec((1, SUBLANES, bw), lambda l, n: (l, 0, n)),
        compiler_params=_cparams(2, 32),
        name="ada_mod",
    )(cc, w_ada, b_ada.reshape(depth, 1, d3))


def _swap_halves(x, half):
    out = []
    lane = lax.broadcasted_iota(jnp.int32, (1, 128), 1)
    first = (lane % (2 * half)) < half
    for s in range(x.shape[1] // 128):
        xs = x[:, 128 * s:128 * (s + 1)]
        fwd = pltpu.roll(xs, 128 - half, axis=1)
        bwd = pltpu.roll(xs, half, axis=1)
        out.append(jnp.where(first, fwd, bwd))
    return out[0] if len(out) == 1 else jnp.concatenate(out, axis=1)


def _tile_lanes(t, reps):
    return t if reps == 1 else jnp.concatenate([t] * reps, axis=1)


def _head_meansq(x):
    w = x.shape[1]
    r = lax.broadcasted_iota(jnp.int32, (w, w), 0) // HEAD
    c = lax.broadcasted_iota(jnp.int32, (w, w), 1) // HEAD
    bd = jnp.where(r == c, 1.0, 0.0).astype(BF16)
    xx = x * x
    hi = xx.astype(BF16)
    lo = (xx - hi.astype(F32)).astype(BF16)
    return (_dot(hi, bd) + _dot(lo, bd)) * (1.0 / HEAD)


def _inproj_kernel(x_ref, ctx_ref, mod_ref, w_ref, cosa_ref, sina_ref, cosb_ref, sinb_ref, qg_ref, kg_ref,
                   qat_ref, ka_ref, vat_ref, qbt_ref, kb_ref, vbt_ref, qct_ref, kc_ref, vct_ref,
                   gate_ref, du_ref, *, n_batch, d_model):
    b = pl.program_id(0)
    t = pl.program_id(1)
    row = jnp.where(t == 0, n_batch, b)
    mod = mod_ref[pl.ds(row, 1), :]
    shift = mod[:, :d_model]
    scale = mod[:, d_model:2 * d_model]
    xin = jnp.where(t == 0, ctx_ref[0], x_ref[0])
    h = (xin * (1.0 + scale) + shift).astype(BF16)
    z = _dot(h, w_ref[...])

    cosa, sina = cosa_ref[...], sina_ref[...]
    cosb, sinb = cosb_ref[...], sinb_ref[...]

    aq = z[:, 0:256]
    aq = aq * lax.rsqrt(_head_meansq(aq) + EPS_RMS) * _tile_lanes(qg_ref[...], 2)
    aq = aq * _tile_lanes(cosa, 2) + _swap_halves(aq, 16) * _tile_lanes(sina, 2)
    qat_ref[0] = (aq * (HEAD ** -0.5 * LOG2E)).T.astype(BF16)
    ak = z[:, 256:384]
    ak = ak * lax.rsqrt(_head_meansq(ak) + EPS_RMS) * kg_ref[...]
    ak = ak * cosa + _swap_halves(ak, 16) * sina
    ka_ref[0] = ak.astype(BF16)
    vat_ref[0] = z[:, 384:512].T.astype(BF16)

    bq = z[:, 768:1024]
    bq = bq * _tile_lanes(cosb, 2) + _swap_halves(bq, 8) * _tile_lanes(sinb, 2)
    qbt_ref[0] = (bq * ((HEAD // 2) ** -0.5 * LOG2E)).T.astype(BF16)
    bk = z[:, 1024:1280]
    bk = bk * _tile_lanes(cosb, 2) + _swap_halves(bk, 8) * _tile_lanes(sinb, 2)
    kb_ref[0] = bk.astype(BF16)
    vbt_ref[0] = z[:, 1280:1536].T.astype(BF16)

    qct_ref[0] = (z[:, 1792:2048] * (HEAD ** -0.5 * LOG2E)).T.astype(BF16)
    kc_ref[0] = z[:, 2048:2304].astype(BF16)
    vct_ref[0] = z[:, 2304:2560].T.astype(BF16)

    du_ref[0] = z[:, 2816:3072].astype(BF16)
    for i, c0 in enumerate((512, 1536, 2560, 3072)):
        g = z[:, c0:c0 + 256]
        gate_ref[0, :, 256 * i:256 * (i + 1)] = (g * jax.nn.sigmoid(g)).astype(BF16)


def _inproj_call(x, ctx, mod, w_in, tabs, qg, kg):
    nb, seq, d = x.shape
    t_all = seq + ctx.shape[1]
    nt = t_all // TOK
    win = w_in.shape[1]
    tok_major = lambda w: (jax.ShapeDtypeStruct((nb, t_all, w), BF16),
                           pl.BlockSpec((1, TOK, w), lambda b, t: (b, t, 0)))
    feat_major = lambda w: (jax.ShapeDtypeStruct((nb, w, t_all), BF16),
                            pl.BlockSpec((1, w, TOK), lambda b, t: (b, 0, t)))
    outs = [feat_major(256), tok_major(128), feat_major(128),
            feat_major(256), tok_major(256), feat_major(256),
            feat_major(256), tok_major(256), feat_major(256),
            tok_major(1024), tok_major(256)]
    tab_spec = pl.BlockSpec((TOK, 128), lambda b, t: (t, 0))
    return pl.pallas_call(
        functools.partial(_inproj_kernel, n_batch=nb, d_model=d),
        out_shape=[o[0] for o in outs],
        grid=(nb, nt),
        in_specs=[pl.BlockSpec((1, TOK, d), lambda b, t: (b, jnp.maximum(t - 1, 0), 0)),
                  pl.BlockSpec((1, TOK, d), lambda b, t: (b, 0, 0)),
                  pl.BlockSpec((SUBLANES, 3 * d), lambda b, t: (0, 0)),
                  pl.BlockSpec((d, win), lambda b, t: (0, 0)),
                  tab_spec, tab_spec, tab_spec, tab_spec,
                  pl.BlockSpec((1, 128), lambda b, t: (0, 0)),
                  pl.BlockSpec((1, 128), lambda b, t: (0, 0))],
        out_specs=[o[1] for o in outs],
        compiler_params=_cparams(2, 56),
        name="in_proj",
    )(x, ctx, mod, w_in, *tabs, qg, kg)


def _pad_rows(x, start, total):
    parts = []
    if start:
        parts.append(jnp.zeros((start, x.shape[1]), x.dtype))
    parts.append(x)
    rest = total - start - x.shape[0]
    if rest:
        parts.append(jnp.zeros((rest, x.shape[1]), x.dtype))
    return jnp.concatenate(parts, axis=0) if len(parts) > 1 else x


def _colmax(parts):
    parts = list(parts)
    while len(parts) > 1:
        parts = [jnp.maximum(parts[2 * i], parts[2 * i + 1]) for i in range(len(parts) // 2)] + parts[len(parts) // 2 * 2:]
    y = parts[0]
    while y.shape[0] > SUBLANES:
        half = y.shape[0] // 2
        y = jnp.maximum(y[:half], y[half:])
    return jnp.max(y, axis=0, keepdims=True)


def _flash_maps(k_ref, vt_ref, q_pads, v_rows, latent, t_all, slots):
    bq = q_pads[0].shape[1]
    n = len(q_pads)

    def scores(off, size, slot=None):
        nsub = size // TOK
        ks = [k_ref[0, pl.ds(off + TOK * r, TOK), :] for r in range(nsub)]
        s = [[_dot(ks[r], q_pads[i]) for r in range(nsub)] for i in range(n)]
        cmax = tuple(_colmax(s[i]) for i in range(n))
        if slot is None:
            return s, cmax
        for i in range(n):
            for r in range(nsub):
                slot[i, TOK * r:TOK * (r + 1), :] = s[i][r]
        return cmax

    def accumulate(stats, s, cmax, off, size):
        nsub = size // TOK
        ones = jnp.ones((16, size), BF16)
        v1 = {}
        new = []
        for i in range(n):
            key = (v_rows[i].start, v_rows[i].stop)
            if key not in v1:
                v1[key] = jnp.concatenate([vt_ref[0, v_rows[i], pl.ds(off, size)], ones], axis=0)
            m_new = jnp.maximum(stats[2 * i], cmax[i])
            p = []
            for r in range(nsub):
                blk = s[i][r] if isinstance(s, list) else s[i, TOK * r:TOK * (r + 1), :]
                p.append(jnp.exp2(blk - m_new).astype(BF16))
            p = p[0] if nsub == 1 else jnp.concatenate(p, axis=0)
            acc = stats[2 * i + 1] * jnp.exp2(stats[2 * i] - m_new) + _dot(v1[key], p)
            new += [m_new, acc]
        return tuple(new)

    stats = []
    for _ in range(n):
        stats += [jnp.full((1, bq), NEG, F32), jnp.zeros((HEAD + 16, bq), F32)]
    stats = tuple(stats)
    s_ctx, c_ctx = scores(0, TOK)
    if not latent:
        stats = accumulate(stats, s_ctx, c_ctx, 0, TOK)
    else:
        s0, s1 = slots
        n_lat = (t_all - TOK) // KCHUNK
        assert n_lat % 2 == 0
        lat_off = lambda c: pl.multiple_of(TOK + c * KCHUNK, TOK)
        c0 = scores(TOK, KCHUNK, s0)
        stats = accumulate(stats, s_ctx, c_ctx, 0, TOK)

        def fused(stats, s_cur, c_cur, off_cur, s_nxt, off_nxt):
            nsub = KCHUNK // TOK
            ks = [k_ref[0, pl.ds(off_nxt + TOK * r, TOK), :] for r in range(nsub)]
            ones = jnp.ones((16, KCHUNK), BF16)
            v1 = {}
            new, c_nxt = [], []
            for i in range(n):
                sn = [_dot(ks[r], q_pads[i]) for r in range(nsub)]
                c_nxt.append(_colmax(sn))
                for r in range(nsub):
                    s_nxt[i, TOK * r:TOK * (r + 1), :] = sn[r]
                key = (v_rows[i].start, v_rows[i].stop)
                if key not in v1:
                    v1[key] = jnp.concatenate([vt_ref[0, v_rows[i], pl.ds(off_cur, KCHUNK)], ones], axis=0)
                m_new = jnp.maximum(stats[2 * i], c_cur[i])
                p = jnp.concatenate([jnp.exp2(s_cur[i, TOK * r:TOK * (r + 1), :] - m_new).astype(BF16)
                                     for r in range(nsub)], axis=0)
                acc = stats[2 * i + 1] * jnp.exp2(stats[2 * i] - m_new) + _dot(v1[key], p)
                new += [m_new, acc]
            return tuple(new), tuple(c_nxt)

        def body(k, carry):
            stats, c0 = carry
            stats, c1 = fused(stats, s0, c0, lat_off(2 * k), s1, lat_off(2 * k + 1))
            stats, c0 = fused(stats, s1, c1, lat_off(2 * k + 1), s0, lat_off(2 * k + 2))
            return stats, c0

        stats, c0 = lax.fori_loop(0, n_lat // 2 - 1, body, (stats, c0))
        stats, c1 = fused(stats, s0, c0, lat_off(n_lat - 2), s1, lat_off(n_lat - 1))
        stats = accumulate(stats, s1, c1, lat_off(n_lat - 1), KCHUNK)
    return [stats[2 * i + 1][0:HEAD] * (1.0 / stats[2 * i + 1][HEAD:HEAD + 1]) for i in range(n)]


def _gqa_kernel(qt_ref, k_ref, vt_ref, o_ref, s0_ref, s1_ref, *, t_all):
    def run(latent):
        rep = N_HEADS // A_KV
        q_pads = [_pad_rows(qt_ref[0, HEAD * h:HEAD * (h + 1), :], HEAD * (h // rep), HEAD * A_KV)
                  for h in range(N_HEADS)]
        v_rows = [slice(HEAD * (h // rep), HEAD * (h // rep + 1)) for h in range(N_HEADS)]
        outs = _flash_maps(k_ref, vt_ref, q_pads, v_rows, latent, t_all, (s0_ref, s1_ref))
        o_ref[0] = jnp.concatenate(outs, axis=0).T.astype(BF16)

    j = pl.program_id(1)
    pl.when(j == 0)(lambda: run(False))
    pl.when(j > 0)(lambda: run(True))


def _gqa_call(qat, ka, vat):
    nb, _, t_all = qat.shape
    return pl.pallas_call(
        functools.partial(_gqa_kernel, t_all=t_all),
        out_shape=jax.ShapeDtypeStruct((nb, t_all, 256), BF16),
        grid=(nb, t_all // TOK),
        in_specs=[pl.BlockSpec((1, 256, TOK), lambda b, j: (b, 0, j)),
                  pl.BlockSpec((1, t_all, 128), lambda b, j: (b, 0, 0)),
                  pl.BlockSpec((1, 128, t_all), lambda b, j: (b, 0, 0))],
        out_specs=pl.BlockSpec((1, TOK, 256), lambda b, j: (b, j, 0)),
        scratch_shapes=[pltpu.VMEM((N_HEADS, KCHUNK, TOK), F32)] * 2,
        compiler_params=_cparams(2, 32),
        name="gqa_attn",
    )(qat, ka, vat)


def _diff_kernel(qt_ref, k_ref, vt_ref, lq1_ref, lk1_ref, lq2_ref, lk2_ref, g_ref, o_ref, s0_ref, s1_ref,
                 *, t_all, lam_init):
    lam = (jnp.exp(jnp.sum(lq1_ref[...] * lk1_ref[...], axis=1, keepdims=True))
           - jnp.exp(jnp.sum(lq2_ref[...] * lk2_ref[...], axis=1, keepdims=True)) + lam_init)
    half = HEAD // 2

    def run(latent):
        q_pads, v_rows = [], []
        for h in range(N_HEADS):
            q_pads.append(_pad_rows(qt_ref[0, HEAD * h:HEAD * h + half, :], HEAD * h, HEAD * N_HEADS))
            q_pads.append(_pad_rows(qt_ref[0, HEAD * h + half:HEAD * (h + 1), :], HEAD * h + half,
                                    HEAD * N_HEADS))
            v_rows += [slice(HEAD * h, HEAD * (h + 1))] * 2
        res = _flash_maps(k_ref, vt_ref, q_pads, v_rows, latent, t_all, (s0_ref, s1_ref))
        outs = []
        for h in range(N_HEADS):
            o = res[2 * h] - lam * res[2 * h + 1]
            ms = jnp.mean(o * o, axis=0, keepdims=True)
            outs.append(o * lax.rsqrt(ms + EPS_RMS))
        y = jnp.concatenate(outs, axis=0).T * _tile_lanes(g_ref[...], 2) * (1.0 - lam_init)
        o_ref[0] = y.astype(BF16)

    j = pl.program_id(1)
    pl.when(j == 0)(lambda: run(False))
    pl.when(j > 0)(lambda: run(True))


def _diff_call(qbt, kb, vbt, lq1, lk1, lq2, lk2, g128, lam_init):
    nb, _, t_all = qbt.shape
    vec = pl.BlockSpec((1, HEAD // 2), lambda b, j: (0, 0))
    return pl.pallas_call(
        functools.partial(_diff_kernel, t_all=t_all, lam_init=lam_init),
        out_shape=jax.ShapeDtypeStruct((nb, t_all, 256), BF16),
        grid=(nb, t_all // TOK),
        in_specs=[pl.BlockSpec((1, 256, TOK), lambda b, j: (b, 0, j)),
                  pl.BlockSpec((1, t_all, 256), lambda b, j: (b, 0, 0)),
                  pl.BlockSpec((1, 256, t_all), lambda b, j: (b, 0, 0)),
                  vec, vec, vec, vec,
                  pl.BlockSpec((1, 128), lambda b, j: (0, 0))],
        out_specs=pl.BlockSpec((1, TOK, 256), lambda b, j: (b, j, 0)),
        scratch_shapes=[pltpu.VMEM((2 * N_HEADS, KCHUNK, TOK), F32)] * 2,
        compiler_params=_cparams(2, 40),
        name="diff_attn",
    )(qbt, kb, vbt, lq1, lk1, lq2, lk2, g128)


NA_WIN = 3


def _na_kernel(qt_ref, k0_ref, k1_ref, k2_ref, kc_ref, v0_ref, v1_ref, v2_ref, vc_ref, bias_ref, o_ref):
    kwin = jnp.concatenate([k0_ref[0], k1_ref[0], k2_ref[0]], axis=0)
    kctx = kc_ref[0]
    ones = jnp.ones((16, TOK), BF16)
    heads = range(N_HEADS)
    rows = [slice(HEAD * h, HEAD * (h + 1)) for h in heads]
    def head_scores(h):
        qpad = _pad_rows(qt_ref[0, rows[h], :], HEAD * h, HEAD * N_HEADS)
        s_win = _dot(kwin, qpad) + bias_ref[h, 0]
        s_ctx = _dot(kctx, qpad)
        m = _colmax([s_win[TOK * r:TOK * (r + 1)] for r in range(NA_WIN)] + [s_ctx])
        return s_win, s_ctx, m

    outs = []
    nxt = head_scores(0)
    for h in heads:
        s_win, s_ctx, m = nxt
        if h + 1 < N_HEADS:
            nxt = head_scores(h + 1)
        p_win = jnp.exp2(s_win - m).astype(BF16)
        p_ctx = jnp.exp2(s_ctx - m).astype(BF16)
        vwin = jnp.concatenate([v0_ref[0, rows[h], :], v1_ref[0, rows[h], :], v2_ref[0, rows[h], :]], axis=1)
        acc = (_dot(jnp.concatenate([vwin, jnp.concatenate([ones] * NA_WIN, axis=1)], axis=0), p_win)
               + _dot(jnp.concatenate([vc_ref[0, rows[h], :], ones], axis=0), p_ctx))
        outs.append(acc[0:HEAD] * (1.0 / acc[HEAD:HEAD + 1]))
    o_ref[0] = jnp.concatenate(outs, axis=0).T.astype(BF16)


def _na_call(qct, kc, vct, bias):
    nb, _, t_all = qct.shape
    nt = t_all // TOK
    n_lat = nt - 1

    def win(i):
        return lambda b, j: 1 + jnp.clip(j - 2, 0, n_lat - NA_WIN) + i

    def pattern(b, j):
        return jnp.where(j == 0, 3, jnp.where(j == 1, 0, jnp.where(j == nt - 1, 2, 1)))

    k_specs = [pl.BlockSpec((1, TOK, 256), (lambda b, j, f=win(i): (b, f(b, j), 0))) for i in range(NA_WIN)]
    v_specs = [pl.BlockSpec((1, 256, TOK), (lambda b, j, f=win(i): (b, 0, f(b, j)))) for i in range(NA_WIN)]
    return pl.pallas_call(
        _na_kernel,
        out_shape=jax.ShapeDtypeStruct((nb, t_all, 256), BF16),
        grid=(nb, nt),
        in_specs=[pl.BlockSpec((1, 256, TOK), lambda b, j: (b, 0, j))]
                 + k_specs + [pl.BlockSpec((1, TOK, 256), lambda b, j: (b, 0, 0))]
                 + v_specs + [pl.BlockSpec((1, 256, TOK), lambda b, j: (b, 0, 0))]
                 + [pl.BlockSpec((N_HEADS, 1, NA_WIN * TOK, TOK), lambda b, j: (0, pattern(b, j), 0, 0))],
        out_specs=pl.BlockSpec((1, TOK, 256), lambda b, j: (b, j, 0)),
        compiler_params=_cparams(2, 48),
        name="na_attn",
    )(qct, kc, kc, kc, kc, vct, vct, vct, vct, bias)


def _na_bias_kernel(t1_ref, o_ref, *, plans):
    neg = jnp.full((GRID_WIDTH, GRID_WIDTH), NEG, F32)
    for p, plan in enumerate(plans):
        for h in range(N_HEADS):
            for kr, drs in enumerate(plan):
                slab = jnp.concatenate([neg if dr is None else t1_ref[0, h, dr] for dr in drs], axis=1)
                o_ref[0, h, p, GRID_WIDTH * kr:GRID_WIDTH * (kr + 1), :] = slab
    o_ref[0, :, len(plans)] = jnp.full((N_HEADS,) + o_ref.shape[3:], NEG, F32)


def _na_bias_all(rpb, n_rows):
    nl, nh = rpb.shape[:2]
    tile_rows = TOK // GRID_WIDTH
    win_rows = NA_WIN * tile_rows
    n_lat = n_rows // tile_rows
    kc = np.arange(GRID_WIDTH)[:, None]
    qc = np.arange(GRID_WIDTH)[None, :]
    cs = np.clip(qc - NA_COLS // 2, 0, GRID_WIDTH - NA_COLS)
    col_ok = (kc >= cs) & (kc < cs + NA_COLS)
    dc = kc - qc + NA_COLS - 1
    e_col = ((dc[None] == np.arange(2 * NA_COLS - 1)[:, None, None]) & col_ok[None]).astype(np.float32)
    t1 = jnp.dot(rpb.reshape(-1, 2 * NA_COLS - 1), e_col.reshape(2 * NA_COLS - 1, -1),
                 precision=lax.Precision.HIGHEST)
    t1 = jnp.where(col_ok.reshape(-1)[None], t1 * LOG2E, NEG)
    t1 = t1.reshape(nl, nh, 2 * NA_ROWS - 1, GRID_WIDTH, GRID_WIDTH)

    def plan(r0, w0):
        out = []
        for kr in range(win_rows):
            drs = []
            for qr in range(tile_rows):
                rs = min(max(r0 + qr - NA_ROWS // 2, 0), n_rows - NA_ROWS)
                ok = rs <= w0 + kr < rs + NA_ROWS
                drs.append((w0 + kr) - (r0 + qr) + NA_ROWS - 1 if ok else None)
            out.append(drs)
        return out

    last = n_lat - 1
    plans = [plan(0, 0), plan(tile_rows, 0), plan(last * tile_rows, (last - NA_WIN + 1) * tile_rows)]
    shape = (nl, nh, len(plans) + 1, win_rows * GRID_WIDTH, TOK)
    return pl.pallas_call(
        functools.partial(_na_bias_kernel, plans=plans),
        out_shape=jax.ShapeDtypeStruct(shape, F32),
        grid=(nl,),
        in_specs=[pl.BlockSpec((1,) + t1.shape[1:], lambda l: (l, 0, 0, 0, 0))],
        out_specs=pl.BlockSpec((1,) + shape[1:], lambda l: (l, 0, 0, 0, 0)),
        compiler_params=_cparams(1, 48),
        name="na_bias",
    )(t1)


def _s5_kernel(u_ref, kt_ref, ws_ref, wot_ref, coef_ref, y_ref, s_scr, ssw_scr, hf_scr, hb_scr, *,
               n_chunks, ctx_chunks):
    tc = S5_CHUNK
    half = SUBLANES // 2
    ut = jnp.concatenate([u_ref[i, 0] for i in range(tc)], axis=0)
    u = ut.astype(F32).T.astype(BF16)
    s_all = _dot(u, ws_ref[0])
    for b in range(half):
        rows = slice(b * n_chunks, (b + 1) * n_chunks)
        s_scr[pl.ds(b, n_chunks, stride=SUBLANES), :] = s_all[rows, 0:128]
        ssw_scr[pl.ds(b, n_chunks, stride=SUBLANES), :] = s_all[rows, 128:256]
        s_scr[pl.ds(half + b, n_chunks, stride=SUBLANES), :] = s_all[rows, 256:384]
        ssw_scr[pl.ds(half + b, n_chunks, stride=SUBLANES), :] = s_all[rows, 384:512]
    c1, c2, c3 = coef_ref[0, 0], coef_ref[0, 1], coef_ref[0, 2]
    is_fwd = lax.broadcasted_iota(jnp.int32, (SUBLANES, 128), 0) < half

    def body(k, carry):
        h, hs = carry
        kb = jnp.where(k < ctx_chunks, ctx_chunks - 1 - k, n_chunks - 1 + ctx_chunks - k)
        rf = pl.multiple_of(k * SUBLANES, SUBLANES)
        rb = pl.multiple_of(kb * SUBLANES, SUBLANES)
        hf_scr[pl.ds(rf, SUBLANES), :] = h
        hb_scr[pl.ds(rb, SUBLANES), :] = h
        sk = jnp.where(is_fwd, s_scr[pl.ds(rf, SUBLANES), :], s_scr[pl.ds(rb, SUBLANES), :])
        ssk = jnp.where(is_fwd, ssw_scr[pl.ds(rf, SUBLANES), :], ssw_scr[pl.ds(rb, SUBLANES), :])
        return h * c1 + hs * c2 + sk, hs * c1 + h * c3 + ssk

    zero = jnp.zeros((SUBLANES, 128), F32)
    lax.fori_loop(0, n_chunks, body, (zero, zero), unroll=4)
    pad = jnp.zeros((u.shape[0] - half * n_chunks, 256), F32)
    hcat = jnp.concatenate(
        [jnp.concatenate([hf_scr[pl.ds(b, n_chunks, stride=SUBLANES), :],
                          hb_scr[pl.ds(half + b, n_chunks, stride=SUBLANES), :]], axis=1)
         for b in range(half)] + ([pad] if pad.shape[0] else []), axis=0)
    nt = (((1,), (1,)), ((), ()))
    y = _dot(kt_ref[0], ut) + lax.dot_general(wot_ref[0], hcat.astype(BF16), nt, preferred_element_type=F32)
    for t in range(tc):
        y_ref[t, 0] = y[tc * t:tc * (t + 1)]


def _s5_call(ut, kt, ws, wot, coef, n_chunks, ctx_chunks):
    tc, g, ch, r = ut.shape
    w = tc * ch
    grp = lambda *shape: pl.BlockSpec((1,) + shape, lambda i: (i,) + (0,) * len(shape))
    io = pl.BlockSpec((tc, 1, ch, r), lambda i: (0, i, 0, 0))
    tile_rows = n_chunks * SUBLANES
    return pl.pallas_call(
        functools.partial(_s5_kernel, n_chunks=n_chunks, ctx_chunks=ctx_chunks),
        out_shape=jax.ShapeDtypeStruct((tc, g, ch, r), F32),
        grid=(g,),
        in_specs=[io, grp(w, w), grp(w, 512), grp(w, w), grp(3, SUBLANES, 128)],
        out_specs=io,
        scratch_shapes=[pltpu.VMEM((tile_rows, 128), F32)] * 4,
        compiler_params=_cparams(1, 48),
        name="s5_scan",
    )(ut, kt, ws, wot, coef)


def _s5_param_kernel(are_ref, aim_ref, ldt_ref, btr_ref, bti_ref, cr_ref, ci_ref,
                     kt_ref, ws_ref, wot_ref, coef_ref):
    tc = S5_CHUNK
    hp = lax.Precision.HIGHEST
    nt = (((1,), (1,)), ((), ()))
    kt_parts, ws_parts, wot_parts, coef_parts = [], [], [], []
    for d in range(2):
        a_re, a_im = are_ref[0, 0, d:d + 1, :], aim_ref[0, 0, d:d + 1, :]
        dt = jnp.exp(ldt_ref[0, 0, d:d + 1, :])
        j = lax.broadcasted_iota(jnp.int32, (2 * tc, 1), 0).astype(F32)
        mag = jnp.exp(j * (a_re * dt))
        ang = j * (a_im * dt)
        pr, pi = mag * jnp.cos(ang), mag * jnp.sin(ang)
        nr, ni = pr[1:2] - 1.0, pi[1:2]
        den = a_re * a_re + a_im * a_im
        fr, fi = (nr * a_re + ni * a_im) / den, (ni * a_re - nr * a_im) / den
        btr, bti = btr_ref[0, 0, d], bti_ref[0, 0, d]
        bbr, bbi = fr * btr - fi * bti, fr * bti + fi * btr
        c_r, c_i = cr_ref[0, 0, d], ci_ref[0, 0, d]
        wr, wi, gr, gi = [], [], [], []
        for pos in range(tc):
            k = tc - 1 - pos if d == 0 else pos
            wr.append(bbr * pr[k:k + 1] - bbi * pi[k:k + 1])
            wi.append(bbr * pi[k:k + 1] + bbi * pr[k:k + 1])
            k = pos + 1 if d == 0 else tc - pos
            gr.append(c_r * pr[k:k + 1] - c_i * pi[k:k + 1])
            gi.append(c_r * pi[k:k + 1] + c_i * pr[k:k + 1])
        wr, wi, gr, gi = (jnp.concatenate(v, axis=0) for v in (wr, wi, gr, gi))
        ws_parts += [wr, wi, wi, wr]
        wot_parts += [gr, -gi]
        row = lax.dot_general(jnp.concatenate([c_r, -c_i], axis=1), jnp.concatenate([wr, wi], axis=1), nt,
                              precision=hp, preferred_element_type=F32)
        ch = row.shape[0]
        blocks = []
        for t in range(tc):
            sh = ch * (tc - 1 - t) if d == 0 else ch * t
            if sh == 0:
                blocks.append(row)
            elif d == 0:
                blocks.append(jnp.concatenate([row[:, sh:], jnp.zeros((ch, sh), F32)], axis=1))
            else:
                blocks.append(jnp.concatenate([jnp.zeros((ch, sh), F32), row[:, :tc * ch - sh]], axis=1))
        kt_parts.append(jnp.concatenate(blocks, axis=0))
        a16r, a16i = pr[tc:tc + 1], pi[tc:tc + 1]
        coef_parts.append([jnp.concatenate([a16r, a16r], axis=1), jnp.concatenate([-a16i, a16i], axis=1),
                           jnp.concatenate([a16i, -a16i], axis=1)])
    kt_ref[0, 0] = (kt_parts[0] + kt_parts[1]).astype(BF16)
    ws_ref[0, 0] = jnp.concatenate(ws_parts, axis=1).astype(BF16)
    wot_ref[0, 0] = jnp.concatenate(wot_parts, axis=1).astype(BF16)
    half = SUBLANES // 2
    for i in range(3):
        coef_ref[0, 0, i] = jnp.concatenate([jnp.broadcast_to(coef_parts[0][i], (half, 128)),
                                             jnp.broadcast_to(coef_parts[1][i], (half, 128))], axis=0)


def _s5_matrices(a_re, a_im, log_dt, b_re, b_im, c_re, c_im):
    nl, _, g, p = a_re.shape
    ch = b_re.shape[-1]
    w = S5_CHUNK * ch
    lgd = lambda a: jnp.moveaxis(a, 1, 2)
    ins = [lgd(a_re), lgd(a_im), lgd(log_dt)[..., None], lgd(jnp.swapaxes(b_re, -1, -2)),
           lgd(jnp.swapaxes(b_im, -1, -2)), lgd(c_re), lgd(c_im)]
    blk = lambda *shape: pl.BlockSpec((1, 1) + shape, lambda l, i: (l, i) + (0,) * len(shape))
    return pl.pallas_call(
        _s5_param_kernel,
        out_shape=[jax.ShapeDtypeStruct((nl, g, w, w), BF16), jax.ShapeDtypeStruct((nl, g, w, 512), BF16),
                   jax.ShapeDtypeStruct((nl, g, w, w), BF16),
                   jax.ShapeDtypeStruct((nl, g, 3, SUBLANES, 128), F32)],
        grid=(nl, g),
        in_specs=[blk(2, p), blk(2, p), blk(2, 1), blk(2, ch, p), blk(2, ch, p), blk(2, ch, p), blk(2, ch, p)],
        out_specs=[blk(w, w), blk(w, 512), blk(w, w), blk(3, SUBLANES, 128)],
        compiler_params=_cparams(2, 32),
        name="s5_params",
    )(*ins)


def _s5_mixer(du, mats, ctx_len):
    nb, t_all, w = du.shape
    tc = S5_CHUNK
    half = SUBLANES // 2
    g = mats[0].shape[0]
    nch = t_all // tc
    r = -(-half * nch // 128) * 128
    u = jnp.pad(du.reshape(nb * nch, tc * w), ((0, r - nb * nch), (0, 0)))
    ut = u.T.reshape(tc, g, w // g, r)
    yt = _s5_call(ut, *mats, nch, ctx_len // tc)
    return yt.reshape(tc * w, r).T[:nb * nch].reshape(nb, t_all, w)


def _outproj_kernel(ya_ref, yb_ref, yc_ref, ys_ref, du_ref, gate_ref, x_ref, ctx_ref, mod_ref, dskip_ref,
                    wglu_ref, wout_ref, lng_ref, lnb_ref, o_ref, octx_ref, *, n_batch, d_model, alpha):
    b = pl.program_id(0)
    t = pl.program_id(1)
    row = jnp.where(t == 0, n_batch, b)
    gate_vec = mod_ref[pl.ds(row, 1), 2 * d_model:3 * d_model]

    yd = ys_ref[0] + dskip_ref[...] * du_ref[0].astype(F32)
    zz = _dot(jax.nn.gelu(yd).astype(BF16), wglu_ref[...])
    yd = zz[:, 0:256] * jax.nn.sigmoid(zz[:, 256:512])
    gates = gate_ref[0]
    parts = [ya_ref[0], yb_ref[0], yc_ref[0]]
    cat = [parts[i] * gates[:, 256 * i:256 * (i + 1)] for i in range(3)]
    cat.append((yd * gates[:, 768:1024].astype(F32)).astype(BF16))
    y = _dot(jnp.concatenate(cat, axis=1), wout_ref[...])
    v = alpha * jnp.where(t == 0, ctx_ref[0], x_ref[0]) + gate_vec * y
    mu = jnp.mean(v, axis=1, keepdims=True)
    vc = v - mu
    var = jnp.mean(vc * vc, axis=1, keepdims=True)
    res = vc * lax.rsqrt(var + EPS_LN) * lng_ref[...] + lnb_ref[...]
    o_ref[0] = res

    @pl.when(t == 0)
    def _():
        octx_ref[0] = res


def _outproj_call(ya, yb, yc, ys, du, gates, x, ctx, mod, dskip, wglu, wout, lng, lnb, alpha):
    nb, seq, d = x.shape
    t_all = seq + ctx.shape[1]
    lat = pl.BlockSpec((1, TOK, d), lambda b, t: (b, jnp.maximum(t - 1, 0), 0))
    cblk = pl.BlockSpec((1, TOK, d), lambda b, t: (b, 0, 0))
    tm = lambda w: pl.BlockSpec((1, TOK, w), lambda b, t: (b, t, 0))
    full = lambda a: pl.BlockSpec(a.shape, lambda b, t: (0,) * a.ndim)
    return pl.pallas_call(
        functools.partial(_outproj_kernel, n_batch=nb, d_model=d, alpha=alpha),
        out_shape=[jax.ShapeDtypeStruct(x.shape, F32), jax.ShapeDtypeStruct(ctx.shape, F32)],
        grid=(nb, t_all // TOK),
        in_specs=[tm(256), tm(256), tm(256), tm(256), tm(256), tm(1024), lat, cblk,
                  full(mod), full(dskip), full(wglu), full(wout), full(lng), full(lnb)],
        out_specs=[lat, cblk],
        compiler_params=pltpu.CompilerParams(dimension_semantics=("parallel", "arbitrary"),
                                             vmem_limit_bytes=40 * MIB),
        name="out_proj",
    )(ya, yb, yc, ys, du, gates, x, ctx, mod, dskip, wglu, wout, lng, lnb)


def _rope_tables(seq, ctx_len):
    t = jnp.arange(seq, dtype=jnp.int32)
    row = (t // GRID_WIDTH).astype(F32)
    col = (t % GRID_WIDTH).astype(F32)

    def tab(half):
        inv = ROPE_THETA ** (-jnp.arange(half, dtype=F32) / half)
        ar, ac = row[:, None] * inv, col[:, None] * inv
        cos = jnp.concatenate([jnp.cos(ar), jnp.cos(ar), jnp.cos(ac), jnp.cos(ac)], axis=1)
        sin = jnp.concatenate([-jnp.sin(ar), jnp.sin(ar), -jnp.sin(ac), jnp.sin(ac)], axis=1)
        cos = jnp.concatenate([jnp.ones((ctx_len, 4 * half), F32), cos], axis=0)
        sin = jnp.concatenate([jnp.zeros((ctx_len, 4 * half), F32), sin], axis=0)
        reps = 128 // (4 * half)
        return jnp.tile(cos, (1, reps)), jnp.tile(sin, (1, reps))

    cosa, sina = tab(HEAD // 4)
    cosb, sinb = tab(HEAD // 8)
    return cosa, sina, cosb, sinb


def kernel(x, c, ctx, c_ctx, w_ada, b_ada, w_in, w_out, ln_g, ln_b, qn_g, kn_g, lam_q1, lam_k1, lam_q2,
           lam_k2, subln_g, na_rpb, s5_a_re, s5_a_im, s5_log_dt, s5_b_re, s5_b_im, s5_c_re, s5_c_im, s5_d,
           w_glu):
    nb, seq, d = x.shape
    ctx_len = ctx.shape[1]
    depth = w_ada.shape[0]
    assert ctx_len == TOK and seq % TOK == 0 and nb < SUBLANES
    alpha = (2 * depth) ** 0.25

    cc = jnp.concatenate([c, c_ctx[None], jnp.zeros((SUBLANES - nb - 1, d), F32)], axis=0)
    mods = _ada_call(cc, w_ada, b_ada)
    tabs = _rope_tables(seq, ctx_len)
    n_rows = seq // GRID_WIDTH

    w_in_b, w_out_b, w_glu_b = w_in.astype(BF16), w_out.astype(BF16), w_glu.astype(BF16)
    na_bias = _na_bias_all(na_rpb, n_rows)
    s5_mats = _s5_matrices(s5_a_re, s5_a_im, s5_log_dt, s5_b_re, s5_b_im, s5_c_re, s5_c_im)
    qg, kg, sg = (jnp.tile(v[:, None], (1, 1, 2)) for v in (qn_g, kn_g, subln_g))

    for l in range(depth):
        lam_init = 0.8 - 0.6 * math.exp(-0.3 * l)
        (qat, ka, vat, qbt, kb, vbt, qct, kc, vct, gates, du) = _inproj_call(
            x, ctx, mods[l], w_in_b[l], tabs, qg[l], kg[l])
        ya = _gqa_call(qat, ka, vat)
        yb = _diff_call(qbt, kb, vbt, lam_q1[l][None], lam_k1[l][None], lam_q2[l][None], lam_k2[l][None],
                        sg[l], lam_init)
        yc = _na_call(qct, kc, vct, na_bias[l])
        ys = _s5_mixer(du, tuple(m[l] for m in s5_mats), ctx_len)
        x, ctx = _outproj_call(ya, yb, yc, ys, du, gates, x, ctx, mods[l], s5_d[l][None], w_glu_b[l],
                               w_out_b[l], ln_g[l][None], ln_b[l][None], alpha)
    return x
```
